```python
import math
import jax, jax.numpy as jnp
from jax import lax
import numpy as np

D_MODEL = 1024
BATCH = 8
SEQ = 2048
DEPTH = 4
DEC_BATCH = 128
DEC_SEQ = 8
PAST_LEN = 16384
PAGE_SIZE = 128

N_META = 16
CHUNK = 64
DN_HEADS = 4
DN_HEAD_DIM = 128
DN_WIDTH = DN_HEADS * DN_HEAD_DIM
CONV_W = 4
ML_HEADS = 4
ML_HEAD_DIM = 128
ML_WIDTH = ML_HEADS * ML_HEAD_DIM
S5_GROUP = 16
S5_WIDTH = 512
S5_GROUPS = S5_WIDTH // S5_GROUP
S5_STATE = 64
N_BRANCH = 3
EPS = 1e-6
NEG = -1e30

IN_SIZES = (3 * DN_WIDTH, DN_WIDTH, DN_HEADS, DN_HEADS,
            3 * ML_WIDTH, ML_WIDTH, ML_WIDTH, ML_HEADS, ML_HEADS,
            S5_WIDTH, S5_WIDTH, N_BRANCH * D_MODEL)
IN_COLS = 4 * DN_WIDTH + 2 * DN_HEADS + 5 * ML_WIDTH + 2 * ML_HEADS + 2 * S5_WIDTH + N_BRANCH * D_MODEL

kernel_name = 'hybrid_gdn_mlstm_s5_decode_step'


def f32(a):
    return a.astype(jnp.float32)


def rms_norm(x, w):
    x32 = x.astype(jnp.float32)
    y = x32 * lax.rsqrt(jnp.mean(x32 * x32, axis=-1, keepdims=True) + EPS)
    return (y * w.astype(jnp.float32)).astype(x.dtype)


def l2_normalize(x):
    return x * lax.rsqrt(jnp.sum(x * x, axis=-1, keepdims=True) + EPS)


def split_cols(a, sizes):
    cuts, acc = [], 0
    for s in sizes[:-1]:
        acc += s
        cuts.append(acc)
    return jnp.split(a, cuts, axis=-1)


def causal_conv(buf, u, w):
    t = u.shape[1]
    up = jnp.concatenate([buf, u], axis=1)
    out = w[0] * up[:, 0:t]
    for j in range(1, CONV_W):
        out = out + w[j] * up[:, j:j + t]
    return jax.nn.silu(out), up[:, t:]


def run_chunks(step, state, xs, is_prompt):
    if not is_prompt:
        return step(state, *xs)
    state, out_meta = step(state, *[a[:, :N_META] for a in xs])
    bsz, t = xs[0].shape[0], xs[0].shape[1]
    n = (t - N_META) // CHUNK

    def to_chunks(a):
        a = a[:, N_META:]
        return jnp.swapaxes(a.reshape((bsz, n, CHUNK) + a.shape[2:]), 0, 1)

    state, out = lax.scan(lambda s, c: step(s, *c), state, tuple(to_chunks(a) for a in xs))
    out = jnp.swapaxes(out, 0, 1).reshape((bsz, n * CHUNK) + out.shape[3:])
    return state, jnp.concatenate([out_meta, out], axis=1)


def gated_delta_chunk(s, q, k, v, g, beta):
    q, k, v = (jnp.swapaxes(a, 1, 2) for a in (q, k, v))
    g, beta = jnp.swapaxes(g, 1, 2), jnp.swapaxes(beta, 1, 2)
    L = q.shape[2]
    tril = jnp.tril(jnp.ones((L, L), dtype=bool))
    strict = jnp.tril(jnp.ones((L, L), dtype=bool), -1)
    G = jnp.cumsum(g, axis=-1)
    decay = jnp.where(tril, jnp.exp(jnp.where(tril, G[..., :, None] - G[..., None, :], 0.0)), 0.0)
    kb = k * beta[..., None]
    a = jnp.where(strict, jnp.einsum('bhik,bhjk->bhij', kb, k) * decay, 0.0)
    lhs = a + jnp.eye(L, dtype=a.dtype)
    rhs = jnp.concatenate([v * beta[..., None], kb * jnp.exp(G)[..., None]], axis=-1)
    sol = lax.linalg.triangular_solve(lhs, rhs, left_side=True, lower=True, unit_diagonal=True)
    dv = v.shape[-1]
    u, w = sol[..., :dv], sol[..., dv:]
    v_new = u - jnp.einsum('bhlk,bhkv->bhlv', w, s)
    attn = jnp.einsum('bhik,bhjk->bhij', q, k) * decay
    o = jnp.einsum('bhlk,bhkv->bhlv', q * jnp.exp(G)[..., None], s) + jnp.einsum('bhij,bhjv->bhiv', attn, v_new)
    g_last = G[..., -1:]
    s = s * jnp.exp(g_last)[..., None] + jnp.einsum('bhlk,bhlv->bhkv', k * jnp.exp(g_last - G)[..., None], v_new)
    return s, jnp.swapaxes(o, 1, 2)


def mlstm_chunk(state, q, k, v, ig, lf):
    c, n, m = state
    q, k, v = (jnp.swapaxes(a, 1, 2) for a in (q, k, v))
    ig, lf = jnp.swapaxes(ig, 1, 2), jnp.swapaxes(lf, 1, 2)
    L = q.shape[2]
    tril = jnp.tril(jnp.ones((L, L), dtype=bool))
    b = jnp.cumsum(lf, axis=-1)
    inter = b + m[..., None]
    logd = jnp.where(tril, b[..., :, None] - b[..., None, :] + ig[..., None, :], NEG)
    m_t = jnp.maximum(inter, jnp.max(logd, axis=-1))
    w_intra = jnp.exp(logd - m_t[..., None])
    w_inter = jnp.exp(inter - m_t)
    sc = jnp.einsum('bhik,bhjk->bhij', q, k) * w_intra
    num = w_inter[..., None] * jnp.einsum('bhlk,bhkv->bhlv', q, c) + jnp.einsum('bhij,bhjv->bhiv', sc, v)
    den = w_inter * jnp.einsum('bhlk,bhk->bhl', q, n) + jnp.sum(sc, axis=-1)
    h = num / jnp.maximum(jnp.abs(den), jnp.exp(-m_t))[..., None]
    m_new = m_t[..., -1]
    w_state = jnp.exp(b[..., -1:] - b + ig - m_new[..., None])
    carry = jnp.exp(b[..., -1] + m - m_new)
    c = carry[..., None, None] * c + jnp.einsum('bhl,bhlk,bhlv->bhkv', w_state, k, v)
    n = carry[..., None] * n + jnp.einsum('bhl,bhlk->bhk', w_state, k)
    return (c, n, m_new), jnp.swapaxes(h, 1, 2)


def ssm_combine(x, y):
    a1, b1 = x
    a2, b2 = y
    return a1 * a2, a2 * b1 + b2


def s5_mixer(h0_re, h0_im, u, lam_re, lam_im, log_dt, b_re, b_im, c_re, c_im, d):
    bsz, t = u.shape[0], u.shape[1]
    lam = lax.complex(lam_re, lam_im)
    lam_bar = jnp.exp(lam * jnp.exp(log_dt)[:, None])
    b_bar = ((lam_bar - 1.0) / lam)[..., None] * lax.complex(b_re, b_im)
    ug = u.reshape(bsz, t, S5_GROUPS, S5_GROUP).astype(jnp.complex64)
    bu = jnp.einsum('gph,btgh->btgp', b_bar, ug)
    bu = bu.at[:, 0].add(lam_bar * lax.complex(h0_re, h0_im))
    a = jnp.broadcast_to(lam_bar, bu.shape)
    _, hs = lax.associative_scan(ssm_combine, (a, bu), axis=1)
    y = jnp.einsum('ghp,btgp->btgh', lax.complex(c_re, c_im), hs).real.reshape(bsz, t, S5_WIDTH) + d * u
    return y, hs[:, -1].real, hs[:, -1].imag


def hybrid_layer(x, state, p, is_prompt):
    conv_buf, dn_s, ml_c, ml_n, ml_m, s5_re, s5_im = state
    (norm_w, w_in, b_gate, dn_conv_w, dn_a_log, dn_dt_bias, dn_norm_w,
     ml_bias_i, ml_bias_f, ml_norm_w, s5_lambda_re, s5_lambda_im, s5_log_dt,
     s5_b_re, s5_b_im, s5_c_re, s5_c_im, s5_d, s5_w_glu, s5_b_glu,
     w_branch_dn, w_branch_ml, w_branch_s5, w_out) = p
    dt = x.dtype
    bsz, t = x.shape[0], x.shape[1]
    h = rms_norm(x, norm_w)
    proj = jnp.einsum('btd,de->bte', h, w_in).astype(jnp.float32)
    (dn_qkv, dn_z, dn_a, dn_b, ml_qkv, ml_o, ml_z, ml_i, ml_f,
     s5_u, s5_z, gate_logits) = split_cols(proj, IN_SIZES)

    qkv, new_conv = causal_conv(conv_buf, dn_qkv, f32(dn_conv_w))
    q, k, v = [a.reshape(bsz, t, DN_HEADS, DN_HEAD_DIM) for a in jnp.split(qkv, 3, axis=-1)]
    q = l2_normalize(q) * (DN_HEAD_DIM ** -0.5)
    k = l2_normalize(k)
    g = -jnp.exp(f32(dn_a_log)) * jax.nn.softplus(dn_a + f32(dn_dt_bias))
    beta = jax.nn.sigmoid(dn_b)
    new_dn_s, o_dn = run_chunks(gated_delta_chunk, dn_s, (q, k, v, g, beta), is_prompt)
    o_dn = rms_norm(o_dn, dn_norm_w) * jax.nn.silu(dn_z.reshape(bsz, t, DN_HEADS, DN_HEAD_DIM))
    o_dn = o_dn.reshape(bsz, t, DN_WIDTH)

    q, k, v = [a.reshape(bsz, t, ML_HEADS, ML_HEAD_DIM) for a in jnp.split(ml_qkv, 3, axis=-1)]
    k = k * (ML_HEAD_DIM ** -0.5)
    ig = ml_i + f32(ml_bias_i)
    lf = jax.nn.log_sigmoid(ml_f + f32(ml_bias_f))
    (new_c, new_n, new_m), h_ml = run_chunks(mlstm_chunk, (ml_c, ml_n, ml_m), (q, k, v, ig, lf), is_prompt)
    o_ml = rms_norm(h_ml, ml_norm_w).reshape(bsz, t, ML_WIDTH) * jax.nn.sigmoid(ml_o) * jax.nn.silu(ml_z)

    y5, new_re, new_im = s5_mixer(s5_re, s5_im, s5_u, f32(s5_lambda_re), f32(s5_lambda_im), f32(s5_log_dt),
                                  f32(s5_b_re), f32(s5_b_im), f32(s5_c_re), f32(s5_c_im), f32(s5_d))
    y5 = jax.nn.gelu(y5)
    y5 = y5 * jax.nn.sigmoid(y5 @ f32(s5_w_glu) + f32(s5_b_glu))
    o_s5 = y5 * jax.nn.silu(s5_z)

    gates = jax.nn.sigmoid(gate_logits + f32(b_gate)).astype(dt).reshape(bsz, t, N_BRANCH, D_MODEL)
    mixed = (gates[:, :, 0] * (o_dn.astype(dt) @ w_branch_dn)
             + gates[:, :, 1] * (o_ml.astype(dt) @ w_branch_ml)
             + gates[:, :, 2] * (o_s5.astype(dt) @ w_branch_s5))
    x = x + mixed @ w_out
    return x, (new_conv, new_dn_s, new_c, new_n, new_m, new_re, new_im)


def run_trunk(x, states, params, is_prompt):
    per_layer = []
    for layer in range(DEPTH):
        st = tuple(s[layer].astype(jnp.float32) for s in states)
        x, new = hybrid_layer(x, st, tuple(w[layer] for w in params), is_prompt)
        per_layer.append(new)
    stacked = tuple(jnp.stack([new[i] for new in per_layer]).astype(x.dtype) for i in range(len(states)))
    return x, stacked


def setup_inputs(seed: int = 0) -> dict:
    key = jax.random.key(seed)
    ks = jax.random.split(key, 40)

    def nrm(i, shape, scale):
        return jax.random.normal(ks[i], shape, jnp.float32) * scale

    def gain(i, shape):
        return 1.0 + nrm(i, shape, 0.01)

    dn_dt = jnp.exp(jax.random.uniform(ks[15], (DEPTH, DN_HEADS), jnp.float32, math.log(1e-3), math.log(1e-1)))
    return {
        'x_prompt': nrm(0, (BATCH, SEQ, D_MODEL), 1.0),
        'x_sample': nrm(1, (DEC_BATCH, DEC_SEQ, D_MODEL), 1.0),
        'state_dn_conv': nrm(2, (DEPTH, DEC_BATCH, CONV_W - 1, 3 * DN_WIDTH), 1.0),
        'state_dn_s': nrm(3, (DEPTH, DEC_BATCH, DN_HEADS, DN_HEAD_DIM, DN_HEAD_DIM), 0.1),
        'state_ml_c': nrm(4, (DEPTH, DEC_BATCH, ML_HEADS, ML_HEAD_DIM, ML_HEAD_DIM), 0.1),
        'state_ml_n': nrm(5, (DEPTH, DEC_BATCH, ML_HEADS, ML_HEAD_DIM), 0.1),
        'state_ml_m': nrm(6, (DEPTH, DEC_BATCH, ML_HEADS), 1.0),
        'state_s5_re': nrm(7, (DEPTH, DEC_BATCH, S5_GROUPS, S5_STATE), 0.5),
        'state_s5_im': nrm(8, (DEPTH, DEC_BATCH, S5_GROUPS, S5_STATE), 0.5),
        'meta_tokens': nrm(9, (N_META, D_MODEL), 1.0),
        'norm_w': gain(10, (DEPTH, D_MODEL)),
        'w_in': nrm(11, (DEPTH, D_MODEL, IN_COLS), D_MODEL ** -0.5),
        'b_gate': nrm(12, (DEPTH, N_BRANCH * D_MODEL), 0.02),
        'dn_conv_w': nrm(13, (DEPTH, CONV_W, 3 * DN_WIDTH), CONV_W ** -0.5),
        'dn_a_log': jnp.log(jax.random.uniform(ks[14], (DEPTH, DN_HEADS), jnp.float32, 1.0, 16.0)),
        'dn_dt_bias': dn_dt + jnp.log(-jnp.expm1(-dn_dt)),
        'dn_norm_w': gain(16, (DEPTH, DN_HEAD_DIM)),
        'ml_bias_i': nrm(17, (DEPTH, ML_HEADS), 0.1),
        'ml_bias_f': jnp.linspace(3.0, 6.0, ML_HEADS, dtype=jnp.float32)[None, :] + nrm(18, (DEPTH, ML_HEADS), 0.1),
        'ml_norm_w': gain(19, (DEPTH, ML_HEAD_DIM)),
        's5_lambda_re': -0.5 + nrm(20, (DEPTH, S5_GROUPS, S5_STATE), 0.01),
        's5_lambda_im': math.pi * jnp.arange(S5_STATE, dtype=jnp.float32) + nrm(21, (DEPTH, S5_GROUPS, S5_STATE), 0.01),
        's5_log_dt': jax.random.uniform(ks[22], (DEPTH, S5_GROUPS), jnp.float32, math.log(1e-3), math.log(1e-1)),
        's5_b_re': nrm(23, (DEPTH, S5_GROUPS, S5_STATE, S5_GROUP), (2 * S5_GROUP) ** -0.5),
        's5_b_im': nrm(24, (DEPTH, S5_GROUPS, S5_STATE, S5_GROUP), (2 * S5_GROUP) ** -0.5),
        's5_c_re': nrm(25, (DEPTH, S5_GROUPS, S5_GROUP, S5_STATE), (2 * S5_STATE) ** -0.5),
        's5_c_im': nrm(26, (DEPTH, S5_GROUPS, S5_GROUP, S5_STATE), (2 * S5_STATE) ** -0.5),
        's5_d': nrm(27, (DEPTH, S5_WIDTH), 1.0),
        's5_w_glu': nrm(28, (DEPTH, S5_WIDTH, S5_WIDTH), S5_WIDTH ** -0.5),
        's5_b_glu': nrm(29, (DEPTH, S5_WIDTH), 0.02),
        'w_branch_dn': nrm(30, (DEPTH, DN_WIDTH, D_MODEL), DN_WIDTH ** -0.5),
        'w_branch_ml': nrm(31, (DEPTH, ML_WIDTH, D_MODEL), ML_WIDTH ** -0.5),
        'w_branch_s5': nrm(32, (DEPTH, S5_WIDTH, D_MODEL), S5_WIDTH ** -0.5),
        'w_out': nrm(33, (DEPTH, D_MODEL, D_MODEL), D_MODEL ** -0.5),
        'final_norm_w': gain(34, (D_MODEL,)),
    }


def reference(x_prompt, x_sample, state_dn_conv, state_dn_s, state_ml_c, state_ml_n, state_ml_m,
              state_s5_re, state_s5_im, meta_tokens, norm_w, w_in, b_gate, dn_conv_w, dn_a_log,
              dn_dt_bias, dn_norm_w, ml_bias_i, ml_bias_f, ml_norm_w, s5_lambda_re, s5_lambda_im,
              s5_log_dt, s5_b_re, s5_b_im, s5_c_re, s5_c_im, s5_d, s5_w_glu, s5_b_glu,
              w_branch_dn, w_branch_ml, w_branch_s5, w_out, final_norm_w):
    params = (norm_w, w_in, b_gate, dn_conv_w, dn_a_log, dn_dt_bias, dn_norm_w,
              ml_bias_i, ml_bias_f, ml_norm_w, s5_lambda_re, s5_lambda_im, s5_log_dt,
              s5_b_re, s5_b_im, s5_c_re, s5_c_im, s5_d, s5_w_glu, s5_b_glu,
              w_branch_dn, w_branch_ml, w_branch_s5, w_out)
    dt = x_prompt.dtype
    bsz = x_prompt.shape[0]

    xp = jnp.concatenate([jnp.broadcast_to(meta_tokens[None].astype(dt), (bsz, N_META, D_MODEL)), x_prompt], axis=1)

    def zeros(*shape):
        return jnp.zeros((DEPTH, bsz) + shape, jnp.float32)

    prompt_init = (zeros(CONV_W - 1, 3 * DN_WIDTH),
                   zeros(DN_HEADS, DN_HEAD_DIM, DN_HEAD_DIM),
                   zeros(ML_HEADS, ML_HEAD_DIM, ML_HEAD_DIM),
                   zeros(ML_HEADS, ML_HEAD_DIM),
                   jnp.full((DEPTH, bsz, ML_HEADS), NEG, jnp.float32),
                   zeros(S5_GROUPS, S5_STATE),
                   zeros(S5_GROUPS, S5_STATE))
    hp, (p_dn_conv, p_dn_s, p_ml_c, p_ml_n, p_ml_m, p_s5_re, p_s5_im) = run_trunk(xp, prompt_init, params, True)
    y_prompt = rms_norm(hp, final_norm_w)[:, N_META:]

    sample_init = (state_dn_conv, state_dn_s, state_ml_c, state_ml_n, state_ml_m, state_s5_re, state_s5_im)
    hs, (s_dn_conv, s_dn_s, s_ml_c, s_ml_n, s_ml_m, s_s5_re, s_s5_im) = run_trunk(x_sample, sample_init, params, False)
    y_sample = rms_norm(hs, final_norm_w)

    return (y_prompt, y_sample,
            p_dn_conv, p_dn_s, p_ml_c, p_ml_n, p_ml_m, p_s5_re, p_s5_im,
            s_dn_conv, s_dn_s, s_ml_c, s_ml_n, s_ml_m, s_s5_re, s_s5_im)
```

```python
import functools

import jax
import jax.numpy as jnp
from jax import lax
from jax.experimental import pallas as pl
from jax.experimental.pallas import tpu as pltpu

F32 = jnp.float32
BF16 = jnp.bfloat16

D_MODEL = 1024
N_META = 16
CHUNK = 64
HEADS = 4
HEAD_DIM = 128
WIDTH = HEADS * HEAD_DIM
CONV_W = 4
S5_GROUP = 16
S5_GROUPS = 32
S5_STATE = 64
S5_COLS = S5_GROUPS * S5_STATE
S5_BLOCKS = 4
S5_BCOLS = S5_COLS // S5_BLOCKS
S5_BIN = WIDTH // S5_BLOCKS
N_BRANCH = 3
NB = 8
LANES = 128
EPS = 1e-6
NEG = -1e30
VMEM_LIMIT_BYTES = 56 * 1024 * 1024

A_DN, A_ML, A_S5, A_SM = 0, 3 * WIDTH, 6 * WIDTH, 7 * WIDTH
A_COLS = 7 * WIDTH + LANES
B_GATE = 4 * WIDTH
B_COLS = 4 * WIDTH + N_BRANCH * D_MODEL
SM_A, SM_B, SM_I, SM_F = 0, HEADS, 2 * HEADS, 3 * HEADS


def _dot(a, b):
    return jnp.dot(a.astype(BF16), b.astype(BF16), preferred_element_type=F32)


def _dot_nt(a, b):
    return lax.dot_general(a.astype(BF16), b.astype(BF16), (((1,), (1,)), ((), ())),
                           preferred_element_type=F32)


def _dot_tn(a, b):
    return lax.dot_general(a.astype(BF16), b.astype(BF16), (((0,), (0,)), ((), ())),
                           preferred_element_type=F32)


def _sigmoid(x):
    return 1.0 / (1.0 + jnp.exp(-x))


def _silu(x):
    return x * _sigmoid(x)


def _softplus(x):
    return jnp.maximum(x, 0.0) + jnp.log1p(jnp.exp(-jnp.abs(x)))


def _gelu_tanh(x):
    return 0.5 * x * (1.0 + jnp.tanh(0.7978845608028654 * (x + 0.044715 * (x * x * x))))


def _rms(x, w):
    return x * lax.rsqrt(jnp.mean(x * x, axis=-1, keepdims=True) + EPS) * w


def _cumsum_rows(tril_b, x):
    hi = x.astype(BF16)
    r1 = x - hi.astype(F32)
    mid = r1.astype(BF16)
    lo = (r1 - mid.astype(F32)).astype(BF16)
    d = functools.partial(jnp.dot, preferred_element_type=F32)
    return d(tril_b, hi) + d(tril_b, mid) + d(tril_b, lo)


def _s5_prep_kernel(lre_ref, lim_ref, ldt_ref, bre_ref, bim_ref, olre_ref, olim_ref, obre_ref, obim_ref):
    lre, lim = lre_ref[...], lim_ref[...]
    dt = jnp.exp(ldt_ref[...])
    mag = jnp.exp(lre * dt)
    lbr = mag * jnp.cos(lim * dt)
    lbi = mag * jnp.sin(lim * dt)
    nr, ni = lbr - 1.0, lbi
    den = lre * lre + lim * lim
    cr = (nr * lre + ni * lim) / den
    ci = (ni * lre - nr * lim) / den
    bre, bim = bre_ref[...], bim_ref[...]
    olre_ref[...] = lbr
    olim_ref[...] = lbi
    obre_ref[...] = cr * bre - ci * bim
    obim_ref[...] = cr * bim + ci * bre


def _s5_prepare(lam_re, lam_im, log_dt, b_re, b_im, c_re, c_im):
    depth = lam_re.shape[0]
    rows = depth * S5_GROUPS * S5_GROUP

    def expand(a):
        return jnp.broadcast_to(a[:, :, None, :], (depth, S5_GROUPS, S5_GROUP, S5_STATE)).reshape(rows, S5_STATE)

    ldt = jnp.broadcast_to(log_dt[:, :, None, None], (depth, S5_GROUPS, S5_GROUP, S5_STATE)).reshape(rows, S5_STATE)
    bt = lambda b: jnp.swapaxes(b, 2, 3).reshape(rows, S5_STATE)
    sds = jax.ShapeDtypeStruct((rows, S5_STATE), F32)
    lbr, lbi, bbr, bbi = pl.pallas_call(
        _s5_prep_kernel, out_shape=(sds, sds, sds, sds), name="s5_prep",
    )(expand(lam_re), expand(lam_im), ldt, bt(b_re), bt(b_im))

    lam_row = lambda a: a.reshape(depth, S5_GROUPS, S5_GROUP, S5_STATE)[:, :, 0, :].reshape(depth, 1, S5_COLS)
    gpb = S5_GROUPS // S5_BLOCKS
    eye = jnp.eye(gpb, dtype=F32)

    def in_map(bb):
        bb = bb.reshape(depth, S5_BLOCKS, gpb, S5_GROUP, S5_STATE)
        w = bb[:, :, :, :, None, :] * eye[None, None, :, None, :, None]
        return w.reshape(depth, S5_BLOCKS, S5_BIN, S5_BCOLS).astype(BF16)

    def out_map(c):
        c = jnp.swapaxes(c, 2, 3).reshape(depth, S5_BLOCKS, gpb, S5_STATE, S5_GROUP)
        w = c[:, :, :, :, None, :] * eye[None, None, :, None, :, None]
        return w.reshape(depth, S5_BLOCKS, S5_BCOLS, S5_BIN).astype(BF16)

    return lam_row(lbr), lam_row(lbi), in_map(bbr), in_map(bbi), out_map(c_re), out_map(c_im)


def _mixer_kernel(*refs, chunk, n_pad, has_init):
    L = chunk
    R = L * NB
    (x_ref, normw_ref, wa_ref, convw_ref, gpar_ref, wbre_ref, wbim_ref, wcre_ref, wcim_ref,
     lamre_ref, lamim_ref, s5d_ref) = refs[:12]
    k = 12
    if has_init:
        conv_in, dns_in, mlc_in, mln_in, mlm_in, s5re_in, s5im_in = refs[k:k + 7]
        k += 7
    o_ref, conv_o, dns_o, mlc_o, mln_o, mlm_o, s5re_o, s5im_o = refs[k:k + 8]
    convbuf, dnslab, mlslab, gslab, u_scr, hsre, hsim = refs[k + 8:]
    step = pl.program_id(0)
    n_state = (CONV_W - 1) * NB

    if has_init:
        convbuf[0:n_state, :] = conv_in[...]
        dns_o[...] = dns_in[...]
        mlc_o[...] = mlc_in[...]
        mln_o[...] = mln_in[...]
        mlm_o[...] = mlm_in[...]
        s5re_o[...] = s5re_in[...]
        s5im_o[...] = s5im_in[...]
    else:
        @pl.when(step == 0)
        def _():
            convbuf[0:n_state, :] = jnp.zeros((n_state, 3 * WIDTH), F32)
            dns_o[...] = jnp.zeros(dns_o.shape, F32)
            mlc_o[...] = jnp.zeros(mlc_o.shape, F32)
            mln_o[...] = jnp.zeros(mln_o.shape, F32)
            mlm_o[...] = jnp.full(mlm_o.shape, NEG, F32)
            s5re_o[...] = jnp.zeros(s5re_o.shape, F32)
            s5im_o[...] = jnp.zeros(s5im_o.shape, F32)

    h = _rms(x_ref[...], normw_ref[...])
    if n_pad:
        row = lax.broadcasted_iota(jnp.int32, (R, 1), 0)
        valid = row >= jnp.where(step == 0, n_pad, 0)
        h = jnp.where(valid, h, 0.0)
    hb = h.astype(BF16)

    for c in range(3):
        convbuf[n_state:, c * WIDTH:(c + 1) * WIDTH] = jnp.dot(
            hb, wa_ref[:, A_DN + c * WIDTH:A_DN + (c + 1) * WIDTH], preferred_element_type=F32)
    for c in range(3):
        p = jnp.dot(hb, wa_ref[:, A_ML + c * WIDTH:A_ML + (c + 1) * WIDTH], preferred_element_type=F32)
        if c == 1:
            p = p * (HEAD_DIM ** -0.5)
        for hh in range(HEADS):
            mlslab[c * HEADS + hh] = p[:, hh * HEAD_DIM:(hh + 1) * HEAD_DIM]
    u_scr[...] = jnp.dot(hb, wa_ref[:, A_S5:A_S5 + WIDTH], preferred_element_type=F32)

    z = jnp.dot(hb, wa_ref[:, A_SM:A_SM + LANES], preferred_element_type=F32) + gpar_ref[1:2, :]
    lane = lax.broadcasted_iota(jnp.int32, (R, LANES), 1)
    ig = z
    if n_pad:
        ig = jnp.where(valid, ig, NEG)
    gates = jnp.where(lane < SM_B, -jnp.exp(gpar_ref[0:1, :]) * _softplus(z),
                      jnp.where(lane < SM_I, _sigmoid(z),
                                jnp.where(lane < SM_F, ig,
                                          jnp.where(lane < SM_F + HEADS, -_softplus(-z), 0.0))))
    gslab[...] = gates

    for c in range(3):
        for hh in range(HEADS):
            col = c * WIDTH + hh * HEAD_DIM
            acc = convw_ref[0:1, col:col + HEAD_DIM] * convbuf[0:R, col:col + HEAD_DIM]
            for j in range(1, CONV_W):
                acc = acc + convw_ref[j:j + 1, col:col + HEAD_DIM] * convbuf[j * NB:j * NB + R, col:col + HEAD_DIM]
            y = _silu(acc)
            if c < 2:
                y = y * lax.rsqrt(jnp.sum(y * y, axis=-1, keepdims=True) + EPS)
            if c == 0:
                y = y * (HEAD_DIM ** -0.5)
            dnslab[c * HEADS + hh] = y
    new_conv = convbuf[R:R + n_state, :]
    conv_o[...] = new_conv
    convbuf[0:n_state, :] = new_conv

    ii = lax.broadcasted_iota(jnp.int32, (L, L), 0)
    jj = lax.broadcasted_iota(jnp.int32, (L, L), 1)
    tril = ii >= jj
    strict = ii > jj
    eye = ii == jj
    tril_b = jnp.where(tril, 1.0, 0.0).astype(BF16)
    lane_row = lax.broadcasted_iota(jnp.int32, (1, LANES), 1)
    n_double = max(1, (L - 1).bit_length())

    def row_of(col):
        return jnp.sum(jnp.where(eye, col, 0.0), axis=0, keepdims=True)

    def seq_body(b, carry):
        rows = pl.ds(b, L, stride=NB)
        gb = gslab[rows, :]
        cs = _cumsum_rows(tril_b, gb)
        m_row = mlm_o[pl.ds(b, 1), :]
        new_m_row = m_row
        for hh in range(HEADS):
            q = dnslab[hh, rows, :]
            kk = dnslab[HEADS + hh, rows, :]
            v = dnslab[2 * HEADS + hh, rows, :]
            gc = cs[:, SM_A + hh:SM_A + hh + 1]
            beta = gb[:, SM_B + hh:SM_B + hh + 1]
            gr = row_of(gc)
            decay = jnp.where(tril, jnp.exp(jnp.where(tril, gc - gr, 0.0)), 0.0)
            eg = jnp.exp(gc)
            kb = kk * beta
            kq = _dot_nt(jnp.concatenate([kb, q], axis=0), kk)
            a = jnp.where(strict, kq[:L] * decay, 0.0)
            attn = kq[L:] * decay
            xs = jnp.concatenate([v * beta, kb * eg], axis=-1)
            pw = -a
            for it in range(n_double):
                xs = xs + _dot(pw, xs)
                if it + 1 < n_double:
                    pw = _dot(pw, pw)
            u = xs[:, :HEAD_DIM]
            w = xs[:, HEAD_DIM:]
            s = dns_o[b, hh]
            sb = s.astype(BF16)
            v_new = u - _dot(w, sb)
            o_ref[hh, rows, :] = _dot(q * eg, sb) + _dot(attn, v_new)
            g_last = cs[L - 1:L, SM_A + hh:SM_A + hh + 1]
            dns_o[b, hh] = s * jnp.exp(g_last) + _dot_tn(kk * jnp.exp(g_last - gc), v_new)

            q = mlslab[hh, rows, :]
            kk = mlslab[HEADS + hh, rows, :]
            v = mlslab[2 * HEADS + hh, rows, :]
            bc = cs[:, SM_F + hh:SM_F + hh + 1]
            igc = gb[:, SM_I + hh:SM_I + hh + 1]
            br = row_of(bc)
            igr = row_of(igc)
            m = m_row[:, hh:hh + 1]
            inter = bc + m
            logd = jnp.where(tril, bc - br + igr, NEG)
            m_t = jnp.maximum(inter, jnp.max(logd, axis=-1, keepdims=True))
            w_intra = jnp.exp(logd - m_t)
            w_inter = jnp.exp(inter - m_t)
            sc = _dot_nt(q, kk) * w_intra
            cm = mlc_o[b, hh]
            nrm = mln_o[b, pl.ds(hh, 1), :]
            num = w_inter * _dot(q, cm) + _dot(sc, v)
            den = w_inter * jnp.sum(q * nrm, axis=-1, keepdims=True) + jnp.sum(sc, axis=-1, keepdims=True)
            o_ref[HEADS + hh, rows, :] = num / jnp.maximum(jnp.abs(den), jnp.exp(-m_t))
            m_new = m_t[L - 1:L, :]
            b_last = bc[L - 1:L, :]
            kw = kk * jnp.exp(b_last - bc + igc - m_new)
            carry_f = jnp.exp(b_last + m - m_new)
            mlc_o[b, hh] = carry_f * cm + _dot_tn(kw, v)
            mln_o[b, pl.ds(hh, 1), :] = carry_f * nrm + jnp.sum(kw, axis=0, keepdims=True)
            new_m_row = jnp.where(lane_row == hh, m_new, new_m_row)
        mlm_o[pl.ds(b, 1), :] = new_m_row
        return carry

    lax.fori_loop(0, NB, seq_body, 0)

    for j in range(S5_BLOCKS):
        cols = slice(j * S5_BCOLS, (j + 1) * S5_BCOLS)
        ucols = slice(j * S5_BIN, (j + 1) * S5_BIN)
        ub = u_scr[:, ucols].astype(BF16)
        hsre[...] = jnp.dot(ub, wbre_ref[j], preferred_element_type=F32)
        hsim[...] = jnp.dot(ub, wbim_ref[j], preferred_element_type=F32)
        are = jnp.broadcast_to(lamre_ref[:, cols], (NB, S5_BCOLS))
        aim = jnp.broadcast_to(lamim_ref[:, cols], (NB, S5_BCOLS))

        def scan_body(t, c, are=are, aim=aim):
            hr, hi = c
            rws = pl.ds(pl.multiple_of(t * NB, NB), NB)
            nr = are * hr - aim * hi + hsre[rws, :]
            ni = are * hi + aim * hr + hsim[rws, :]
            hsre[rws, :] = nr
            hsim[rws, :] = ni
            return nr, ni

        hr, hi = lax.fori_loop(0, L, scan_body, (s5re_o[:, cols], s5im_o[:, cols]))
        s5re_o[:, cols] = hr
        s5im_o[:, cols] = hi
        o_ref[2 * HEADS + j] = (_dot(hsre[...], wcre_ref[j]) - _dot(hsim[...], wcim_ref[j])
                                + s5d_ref[:, ucols] * u_scr[:, ucols])


def _const_spec(shape, index):
    return pl.BlockSpec(shape, lambda s, index=index: index, pipeline_mode=pl.Buffered(1))


def _mixer_call(layer, x, p, init, chunk, n_pad):
    n_steps, R, _ = x.shape
    has_init = init is not None
    n_state = (CONV_W - 1) * NB
    n_seq = n_steps * NB if has_init else NB
    n_blk = n_steps if has_init else 1
    sidx = (lambda s: s) if has_init else (lambda s: 0)

    in_specs = [
        pl.BlockSpec((None, R, D_MODEL), lambda s: (s, 0, 0)),
        _const_spec((None, 1, D_MODEL), (layer, 0, 0)),
        _const_spec((None, D_MODEL, A_COLS), (layer, 0, 0)),
        _const_spec((None, CONV_W, 3 * WIDTH), (layer, 0, 0)),
        _const_spec((None, 2, LANES), (layer, 0, 0)),
        _const_spec((None, S5_BLOCKS, S5_BIN, S5_BCOLS), (layer, 0, 0, 0)),
        _const_spec((None, S5_BLOCKS, S5_BIN, S5_BCOLS), (layer, 0, 0, 0)),
        _const_spec((None, S5_BLOCKS, S5_BCOLS, S5_BIN), (layer, 0, 0, 0)),
        _const_spec((None, S5_BLOCKS, S5_BCOLS, S5_BIN), (layer, 0, 0, 0)),
        _const_spec((None, 1, S5_COLS), (layer, 0, 0)),
        _const_spec((None, 1, S5_COLS), (layer, 0, 0)),
        _const_spec((None, 1, WIDTH), (layer, 0, 0)),
    ]
    args = [x, p["norm_w"], p["wa"], p["conv_w"], p["gpar"], p["wb_re"], p["wb_im"], p["wc_re"], p["wc_im"],
            p["lam_re"], p["lam_im"], p["s5_d"]]
    if has_init:
        in_specs += [
            pl.BlockSpec((None, None, n_state, 3 * WIDTH), lambda s: (layer, s, 0, 0)),
            pl.BlockSpec((None, NB, HEADS, HEAD_DIM, HEAD_DIM), lambda s: (layer, s, 0, 0, 0)),
            pl.BlockSpec((None, NB, HEADS, HEAD_DIM, HEAD_DIM), lambda s: (layer, s, 0, 0, 0)),
            pl.BlockSpec((None, NB, HEADS, HEAD_DIM), lambda s: (layer, s, 0, 0)),
            pl.BlockSpec((None, NB, LANES), lambda s: (layer, s, 0)),
            pl.BlockSpec((None, NB, S5_COLS), lambda s: (layer, s, 0)),
            pl.BlockSpec((None, NB, S5_COLS), lambda s: (layer, s, 0)),
        ]
        args += list(init)

    out_shape = (
        jax.ShapeDtypeStruct((n_steps, 3 * HEADS, R, LANES), F32),
        jax.ShapeDtypeStruct((n_blk, n_state, 3 * WIDTH), F32),
        jax.ShapeDtypeStruct((n_seq, HEADS, HEAD_DIM, HEAD_DIM), F32),
        jax.ShapeDtypeStruct((n_seq, HEADS, HEAD_DIM, HEAD_DIM), F32),
        jax.ShapeDtypeStruct((n_seq, HEADS, HEAD_DIM), F32),
        jax.ShapeDtypeStruct((n_seq, LANES), F32),
        jax.ShapeDtypeStruct((n_seq, S5_COLS), F32),
        jax.ShapeDtypeStruct((n_seq, S5_COLS), F32),
    )
    out_specs = (
        pl.BlockSpec((None, 3 * HEADS, R, LANES), lambda s: (s, 0, 0, 0)),
        pl.BlockSpec((None, n_state, 3 * WIDTH), lambda s: (sidx(s), 0, 0)),
        pl.BlockSpec((NB, HEADS, HEAD_DIM, HEAD_DIM), lambda s: (sidx(s), 0, 0, 0)),
        pl.BlockSpec((NB, HEADS, HEAD_DIM, HEAD_DIM), lambda s: (sidx(s), 0, 0, 0)),
        pl.BlockSpec((NB, HEADS, HEAD_DIM), lambda s: (sidx(s), 0, 0)),
        pl.BlockSpec((NB, LANES), lambda s: (sidx(s), 0)),
        pl.BlockSpec((NB, S5_COLS), lambda s: (sidx(s), 0)),
        pl.BlockSpec((NB, S5_COLS), lambda s: (sidx(s), 0)),
    )
    scratch = [
        pltpu.VMEM((n_state + R, 3 * WIDTH), F32),
        pltpu.VMEM((3 * HEADS, R, LANES), F32),
        pltpu.VMEM((3 * HEADS, R, LANES), F32),
        pltpu.VMEM((R, LANES), F32),
        pltpu.VMEM((R, WIDTH), F32),
        pltpu.VMEM((R, S5_BCOLS), F32),
        pltpu.VMEM((R, S5_BCOLS), F32),
    ]
    return pl.pallas_call(
        functools.partial(_mixer_kernel, chunk=chunk, n_pad=n_pad, has_init=has_init),
        grid=(n_steps,), in_specs=in_specs, out_specs=out_specs, out_shape=out_shape,
        scratch_shapes=scratch, name="mixer_sample" if has_init else "mixer_prompt",
        compiler_params=pltpu.CompilerParams(dimension_semantics=("arbitrary",),
                                             vmem_limit_bytes=VMEM_LIMIT_BYTES),
    )(*args)


def _merge_kernel(*refs, n_pad, final):
    (x_ref, o_ref, normw_ref, wb_ref, bgate_ref, dnnw_ref, mlnw_ref, wglu_ref, bglu_ref,
     wbr_ref, wout_ref) = refs[:11]
    fin_ref = refs[11] if final else None
    out_ref = refs[-1]
    sb, R, _ = x_ref.shape
    rt = sb * R
    x = x_ref[...].reshape(rt, D_MODEL)
    hb = _rms(x, normw_ref[...]).astype(BF16)

    def slab(i):
        return o_ref[:, i].reshape(rt, LANES)

    def head_cols(a, hh):
        return a[:, hh * HEAD_DIM:(hh + 1) * HEAD_DIM]

    zall = jnp.dot(hb, wb_ref[:, 0:B_GATE], preferred_element_type=F32)
    dn_z, ml_o = zall[:, 0:WIDTH], zall[:, WIDTH:2 * WIDTH]
    ml_z, s5_z = zall[:, 2 * WIDTH:3 * WIDTH], zall[:, 3 * WIDTH:4 * WIDTH]

    o_dn = jnp.concatenate(
        [_rms(slab(hh), dnnw_ref[...]) * _silu(head_cols(dn_z, hh)) for hh in range(HEADS)], axis=-1)
    o_ml = jnp.concatenate(
        [_rms(slab(HEADS + hh), mlnw_ref[...]) * _sigmoid(head_cols(ml_o, hh)) * _silu(head_cols(ml_z, hh))
         for hh in range(HEADS)], axis=-1)
    y5 = _gelu_tanh(jnp.concatenate([slab(2 * HEADS + j) for j in range(S5_BLOCKS)], axis=-1))
    y5 = y5 * _sigmoid(_dot(y5, wglu_ref[...]) + bglu_ref[...])
    o_s5 = y5 * _silu(s5_z)

    mixed = None
    for i, ob in enumerate((o_dn, o_ml, o_s5)):
        gcols = slice(i * D_MODEL, (i + 1) * D_MODEL)
        gate = _sigmoid(jnp.dot(hb, wb_ref[:, B_GATE + i * D_MODEL:B_GATE + (i + 1) * D_MODEL],
                                preferred_element_type=F32) + bgate_ref[:, gcols])
        term = gate * _dot(ob, wbr_ref[i])
        mixed = term if mixed is None else mixed + term
    delta = _dot(mixed, wout_ref[...])
    if n_pad:
        row = lax.broadcasted_iota(jnp.int32, (rt, 1), 0)
        delta = jnp.where(row >= jnp.where(pl.program_id(0) == 0, n_pad, 0), delta, 0.0)
    xo = x + delta
    if final:
        xo = _rms(xo, fin_ref[...])
    out_ref[...] = xo.reshape(sb, R, D_MODEL)


def _merge_call(layer, x, o_raw, p, final_w, sb, n_pad):
    n_steps, R, _ = x.shape
    final = final_w is not None
    in_specs = [
        pl.BlockSpec((sb, R, D_MODEL), lambda s: (s, 0, 0)),
        pl.BlockSpec((sb, 3 * HEADS, R, LANES), lambda s: (s, 0, 0, 0)),
        _const_spec((None, 1, D_MODEL), (layer, 0, 0)),
        _const_spec((None, D_MODEL, B_COLS), (layer, 0, 0)),
        _const_spec((None, 1, N_BRANCH * D_MODEL), (layer, 0, 0)),
        _const_spec((None, 1, HEAD_DIM), (layer, 0, 0)),
        _const_spec((None, 1, HEAD_DIM), (layer, 0, 0)),
        _const_spec((None, WIDTH, WIDTH), (layer, 0, 0)),
        _const_spec((None, 1, WIDTH), (layer, 0, 0)),
        _const_spec((None, N_BRANCH, WIDTH, D_MODEL), (layer, 0, 0, 0)),
        _const_spec((None, D_MODEL, D_MODEL), (layer, 0, 0)),
    ]
    args = [x, o_raw, p["norm_w"], p["wb"], p["b_gate"], p["dn_norm_w"], p["ml_norm_w"], p["w_glu"], p["b_glu"],
            p["w_branch"], p["w_out"]]
    if final:
        in_specs.append(_const_spec((1, D_MODEL), (0, 0)))
        args.append(final_w)
    return pl.pallas_call(
        functools.partial(_merge_kernel, n_pad=n_pad, final=final),
        grid=(n_steps // sb,), in_specs=in_specs,
        out_specs=pl.BlockSpec((sb, R, D_MODEL), lambda s: (s, 0, 0)),
        out_shape=jax.ShapeDtypeStruct(x.shape, F32), name="merge",
        compiler_params=pltpu.CompilerParams(dimension_semantics=("arbitrary",),
                                             vmem_limit_bytes=VMEM_LIMIT_BYTES),
    )(*args)


def _time_major(a, nb=NB):
    n, t = a.shape[0], a.shape[1]
    a = a.reshape((n // nb, nb, t) + a.shape[2:])
    a = jnp.swapaxes(a, 1, 2)
    return a.reshape((n // nb, t * nb) + a.shape[3:])


def _seq_major(a, t, nb=NB):
    g = a.shape[0]
    a = a.reshape((g, t, nb) + a.shape[2:])
    a = jnp.swapaxes(a, 1, 2)
    return a.reshape((g * nb, t) + a.shape[3:])


def kernel(x_prompt, x_sample, state_dn_conv, state_dn_s, state_ml_c, state_ml_n, state_ml_m, state_s5_re, state_s5_im, meta_tokens, norm_w, w_in, b_gate, dn_conv_w, dn_a_log, dn_dt_bias, dn_norm_w, ml_bias_i, ml_bias_f, ml_norm_w, s5_lambda_re, s5_lambda_im, s5_log_dt, s5_b_re, s5_b_im, s5_c_re, s5_c_im, s5_d, s5_w_glu, s5_b_glu, w_branch_dn, w_branch_ml, w_branch_s5, w_out, final_norm_w):
    depth = w_in.shape[0]
    bsz, seq, _ = x_prompt.shape
    dec_b, dec_t, _ = x_sample.shape
    assert bsz == NB and dec_b % NB == 0 and seq % CHUNK == 0 and dec_t >= CONV_W - 1

    cuts = [0]
    for sz in (3 * WIDTH, WIDTH, HEADS, HEADS, 3 * WIDTH, WIDTH, WIDTH, HEADS, HEADS, WIDTH, WIDTH, N_BRANCH * D_MODEL):
        cuts.append(cuts[-1] + sz)
    seg = lambda i: w_in[:, :, cuts[i]:cuts[i + 1]]
    (dn_qkv, dn_z, dn_a, dn_b, ml_qkv, ml_o, ml_z, ml_i, ml_f, s5_u, s5_z, gate_w) = [seg(i) for i in range(12)]
    small = jnp.concatenate([dn_a, dn_b, ml_i, ml_f, jnp.zeros((depth, D_MODEL, LANES - 4 * HEADS), F32)], axis=-1)
    wa = jnp.concatenate([dn_qkv, ml_qkv, s5_u, small], axis=-1).astype(BF16)
    wb = jnp.concatenate([dn_z, ml_o, ml_z, s5_z, gate_w], axis=-1).astype(BF16)

    def lanes_row(parts):
        row = jnp.zeros((depth, LANES), F32)
        for off, val in parts:
            row = row.at[:, off:off + HEADS].set(val)
        return row

    gpar = jnp.stack([lanes_row([(SM_A, dn_a_log)]),
                      lanes_row([(SM_A, dn_dt_bias), (SM_I, ml_bias_i), (SM_F, ml_bias_f)])], axis=1)
    lam_re, lam_im, wb_re, wb_im, wc_re, wc_im = _s5_prepare(
        s5_lambda_re, s5_lambda_im, s5_log_dt, s5_b_re, s5_b_im, s5_c_re, s5_c_im)
    p = dict(
        norm_w=norm_w[:, None, :], wa=wa, wb=wb, conv_w=dn_conv_w, gpar=gpar,
        wb_re=wb_re, wb_im=wb_im, wc_re=wc_re, wc_im=wc_im, lam_re=lam_re, lam_im=lam_im,
        s5_d=s5_d[:, None, :], b_gate=b_gate[:, None, :], dn_norm_w=dn_norm_w[:, None, :],
        ml_norm_w=ml_norm_w[:, None, :], w_glu=s5_w_glu.astype(BF16), b_glu=s5_b_glu[:, None, :],
        w_branch=jnp.stack([w_branch_dn, w_branch_ml, w_branch_s5], axis=1).astype(BF16),
        w_out=w_out.astype(BF16))
    fin = final_norm_w[None, :]

    n_front = CHUNK - N_META
    xp = jnp.concatenate([jnp.zeros((bsz, n_front, D_MODEL), F32),
                          jnp.broadcast_to(meta_tokens[None], (bsz, N_META, D_MODEL)), x_prompt], axis=1)
    t_all = xp.shape[1]
    xp = jnp.swapaxes(xp, 0, 1).reshape(t_all // CHUNK, CHUNK * NB, D_MODEL)
    xs = _time_major(x_sample)

    n_grp = dec_b // NB
    conv_in = jnp.swapaxes(state_dn_conv.reshape(depth, n_grp, NB, CONV_W - 1, 3 * WIDTH), 2, 3)
    conv_in = conv_in.reshape(depth, n_grp, (CONV_W - 1) * NB, 3 * WIDTH)
    mlm_in = jnp.pad(state_ml_m, ((0, 0), (0, 0), (0, LANES - HEADS)))
    init = (conv_in, state_dn_s, state_ml_c, state_ml_n, mlm_in,
            state_s5_re.reshape(depth, dec_b, S5_COLS), state_s5_im.reshape(depth, dec_b, S5_COLS))

    p_states, s_states = [], []
    for layer in range(depth):
        last = layer == depth - 1
        res = _mixer_call(layer, xp, p, None, CHUNK, n_front * NB)
        p_states.append(res[1:])
        xp = _merge_call(layer, xp, res[0], p, fin if last else None, 1, n_front * NB)
        res = _mixer_call(layer, xs, p, init, dec_t, 0)
        s_states.append(res[1:])
        xs = _merge_call(layer, xs, res[0], p, fin if last else None, n_grp, 0)

    y_prompt = jnp.swapaxes(xp.reshape(t_all, bsz, D_MODEL), 0, 1)[:, CHUNK:]
    y_sample = _seq_major(xs, dec_t)

    def collect(states):
        st = [jnp.stack([s[i] for s in states]) for i in range(7)]
        conv, dns, mlc, mln, mlm, s5r, s5i = st
        n_seq = dns.shape[1]
        conv = conv.reshape(depth, n_seq // NB, CONV_W - 1, NB, 3 * WIDTH)
        conv = jnp.swapaxes(conv, 2, 3).reshape(depth, n_seq, CONV_W - 1, 3 * WIDTH)
        return (conv, dns, mlc, mln, mlm[:, :, :HEADS],
                s5r.reshape(depth, n_seq, S5_GROUPS, S5_STATE), s5i.reshape(depth, n_seq, S5_GROUPS, S5_STATE))

    return (y_prompt, y_sample) + collect(p_states) + collect(s_states)
```

```python
import functools

import jax
import jax.numpy as jnp
from jax import lax
from jax.experimental import pallas as pl
from jax.experimental.pallas import tpu as pltpu

F32 = jnp.float32
BF16 = jnp.bfloat16

D_MODEL = 1024
N_META = 16
CHUNK = 64
HEADS = 4
HEAD_DIM = 128
WIDTH = HEADS * HEAD_DIM
CONV_W = 4
S5_GROUP = 16
S5_GROUPS = 32
S5_STATE = 64
S5_COLS = S5_GROUPS * S5_STATE
S5_BLOCKS = 4
S5_BCOLS = S5_COLS // S5_BLOCKS
S5_BIN = WIDTH // S5_BLOCKS
N_BRANCH = 3
NB = 8
SEQ_PER_ITER = 2
LANES = 128
EPS = 1e-6
NEG = -1e30
VMEM_LIMIT_BYTES = 56 * 1024 * 1024

A_DN, A_ML, A_S5, A_SM = 0, 3 * WIDTH, 6 * WIDTH, 7 * WIDTH
A_COLS = 7 * WIDTH + LANES
B_GATE = 4 * WIDTH
B_COLS = 4 * WIDTH + N_BRANCH * D_MODEL
SM_A, SM_B, SM_I, SM_F = 0, HEADS, 2 * HEADS, 3 * HEADS


def _dot(a, b):
    return jnp.dot(a.astype(BF16), b.astype(BF16), preferred_element_type=F32)


def _dot_nt(a, b):
    return lax.dot_general(a.astype(BF16), b.astype(BF16), (((1,), (1,)), ((), ())),
                           preferred_element_type=F32)


def _dot_tn(a, b):
    return lax.dot_general(a.astype(BF16), b.astype(BF16), (((0,), (0,)), ((), ())),
                           preferred_element_type=F32)


def _sigmoid(x):
    return 1.0 / (1.0 + jnp.exp(-x))


def _silu(x):
    return x * _sigmoid(x)


def _softplus(x):
    return jnp.maximum(x, 0.0) + jnp.log1p(jnp.exp(-jnp.abs(x)))


def _gelu_tanh(x):
    return 0.5 * x * (1.0 + jnp.tanh(0.7978845608028654 * (x + 0.044715 * (x * x * x))))


def _rms(x, w):
    return x * lax.rsqrt(jnp.mean(x * x, axis=-1, keepdims=True) + EPS) * w


def _cumsum_rows(tril_b, x):
    hi = x.astype(BF16)
    r1 = x - hi.astype(F32)
    mid = r1.astype(BF16)
    lo = (r1 - mid.astype(F32)).astype(BF16)
    d = functools.partial(jnp.dot, preferred_element_type=F32)
    return d(tril_b, hi) + d(tril_b, mid) + d(tril_b, lo)


def _s5_prep_kernel(lre_ref, lim_ref, ldt_ref, bre_ref, bim_ref, olre_ref, olim_ref, obre_ref, obim_ref):
    lre, lim = lre_ref[...], lim_ref[...]
    dt = jnp.exp(ldt_ref[...])
    mag = jnp.exp(lre * dt)
    lbr = mag * jnp.cos(lim * dt)
    lbi = mag * jnp.sin(lim * dt)
    nr, ni = lbr - 1.0, lbi
    den = lre * lre + lim * lim
    cr = (nr * lre + ni * lim) / den
    ci = (ni * lre - nr * lim) / den
    bre, bim = bre_ref[...], bim_ref[...]
    olre_ref[...] = lbr
    olim_ref[...] = lbi
    obre_ref[...] = cr * bre - ci * bim
    obim_ref[...] = cr * bim + ci * bre


def _s5_prepare(lam_re, lam_im, log_dt, b_re, b_im, c_re, c_im):
    depth = lam_re.shape[0]
    rows = depth * S5_GROUPS * S5_GROUP

    def expand(a):
        return jnp.broadcast_to(a[:, :, None, :], (depth, S5_GROUPS, S5_GROUP, S5_STATE)).reshape(rows, S5_STATE)

    ldt = jnp.broadcast_to(log_dt[:, :, None, None], (depth, S5_GROUPS, S5_GROUP, S5_STATE)).reshape(rows, S5_STATE)
    bt = lambda b: jnp.swapaxes(b, 2, 3).reshape(rows, S5_STATE)
    sds = jax.ShapeDtypeStruct((rows, S5_STATE), F32)
    lbr, lbi, bbr, bbi = pl.pallas_call(
        _s5_prep_kernel, out_shape=(sds, sds, sds, sds), name="s5_prep",
    )(expand(lam_re), expand(lam_im), ldt, bt(b_re), bt(b_im))

    lam_row = lambda a: a.reshape(depth, S5_GROUPS, S5_GROUP, S5_STATE)[:, :, 0, :].reshape(depth, 1, S5_COLS)
    gpb = S5_GROUPS // S5_BLOCKS
    eye = jnp.eye(gpb, dtype=F32)

    def in_map(bb):
        bb = bb.reshape(depth, S5_BLOCKS, gpb, S5_GROUP, S5_STATE)
        w = bb[:, :, :, :, None, :] * eye[None, None, :, None, :, None]
        return w.reshape(depth, S5_BLOCKS, S5_BIN, S5_BCOLS).astype(BF16)

    def out_map(c):
        c = jnp.swapaxes(c, 2, 3).reshape(depth, S5_BLOCKS, gpb, S5_STATE, S5_GROUP)
        w = c[:, :, :, :, None, :] * eye[None, None, :, None, :, None]
        return w.reshape(depth, S5_BLOCKS, S5_BCOLS, S5_BIN).astype(BF16)

    return lam_row(lbr), lam_row(lbi), in_map(bbr), in_map(bbi), out_map(c_re), out_map(c_im)


def _mixer_kernel(*refs, chunk, n_pad, has_init):
    L = chunk
    R = L * NB
    (x_ref, normw_ref, wa_ref, convw_ref, gpar_ref, wbre_ref, wbim_ref, wcre_ref, wcim_ref,
     lamre_ref, lamim_ref, s5d_ref) = refs[:12]
    k = 12
    if has_init:
        conv_in, dns_in, mlc_in, mln_in, mlm_in, s5re_in, s5im_in = refs[k:k + 7]
        k += 7
    o_ref, conv_o, dns_o, mlc_o, mln_o, mlm_o, s5re_o, s5im_o = refs[k:k + 8]
    convbuf, dnslab, mlslab, gslab, u_scr, hsre, hsim = refs[k + 8:]
    step = pl.program_id(0)
    n_state = (CONV_W - 1) * NB

    if has_init:
        convbuf[0:n_state, :] = conv_in[...]
        dns_o[...] = dns_in[...]
        mlc_o[...] = mlc_in[...]
        mln_o[...] = mln_in[...]
        mlm_o[...] = mlm_in[...]
        s5re_o[...] = s5re_in[...]
        s5im_o[...] = s5im_in[...]
    else:
        @pl.when(step == 0)
        def _():
            convbuf[0:n_state, :] = jnp.zeros((n_state, 3 * WIDTH), F32)
            dns_o[...] = jnp.zeros(dns_o.shape, F32)
            mlc_o[...] = jnp.zeros(mlc_o.shape, F32)
            mln_o[...] = jnp.zeros(mln_o.shape, F32)
            mlm_o[...] = jnp.full(mlm_o.shape, NEG, F32)
            s5re_o[...] = jnp.zeros(s5re_o.shape, F32)
            s5im_o[...] = jnp.zeros(s5im_o.shape, F32)

    h = _rms(x_ref[...], normw_ref[...])
    if n_pad:
        row = lax.broadcasted_iota(jnp.int32, (R, 1), 0)
        valid = row >= jnp.where(step == 0, n_pad, 0)
        h = jnp.where(valid, h, 0.0)
    hb = h.astype(BF16)

    for c in range(3):
        convbuf[n_state:, c * WIDTH:(c + 1) * WIDTH] = jnp.dot(
            hb, wa_ref[:, A_DN + c * WIDTH:A_DN + (c + 1) * WIDTH], preferred_element_type=F32)
    for c in range(3):
        p = jnp.dot(hb, wa_ref[:, A_ML + c * WIDTH:A_ML + (c + 1) * WIDTH], preferred_element_type=F32)
        if c == 1:
            p = p * (HEAD_DIM ** -0.5)
        for hh in range(HEADS):
            mlslab[c * HEADS + hh] = p[:, hh * HEAD_DIM:(hh + 1) * HEAD_DIM]
    u_scr[...] = jnp.dot(hb, wa_ref[:, A_S5:A_S5 + WIDTH], preferred_element_type=F32)

    z = jnp.dot(hb, wa_ref[:, A_SM:A_SM + LANES], preferred_element_type=F32) + gpar_ref[1:2, :]
    lane = lax.broadcasted_iota(jnp.int32, (R, LANES), 1)
    ig = z
    if n_pad:
        ig = jnp.where(valid, ig, NEG)
    gates = jnp.where(lane < SM_B, -jnp.exp(gpar_ref[0:1, :]) * _softplus(z),
                      jnp.where(lane < SM_I, _sigmoid(z),
                                jnp.where(lane < SM_F, ig,
                                          jnp.where(lane < SM_F + HEADS, -_softplus(-z), 0.0))))
    gslab[...] = gates

    for c in range(3):
        for hh in range(HEADS):
            col = c * WIDTH + hh * HEAD_DIM
            acc = convw_ref[0:1, col:col + HEAD_DIM] * convbuf[0:R, col:col + HEAD_DIM]
            for j in range(1, CONV_W):
                acc = acc + convw_ref[j:j + 1, col:col + HEAD_DIM] * convbuf[j * NB:j * NB + R, col:col + HEAD_DIM]
            y = _silu(acc)
            if c < 2:
                y = y * lax.rsqrt(jnp.sum(y * y, axis=-1, keepdims=True) + EPS)
            if c == 0:
                y = y * (HEAD_DIM ** -0.5)
            dnslab[c * HEADS + hh] = y
    new_conv = convbuf[R:R + n_state, :]
    conv_o[...] = new_conv
    convbuf[0:n_state, :] = new_conv

    ii = lax.broadcasted_iota(jnp.int32, (L, L), 0)
    jj = lax.broadcasted_iota(jnp.int32, (L, L), 1)
    tril = ii >= jj
    strict = ii > jj
    eye = ii == jj
    tril_b = jnp.where(tril, 1.0, 0.0).astype(BF16)
    lane_row = lax.broadcasted_iota(jnp.int32, (1, LANES), 1)
    n_double = max(1, (L - 1).bit_length())

    def row_of(col):
        return jnp.sum(jnp.where(eye, col, 0.0), axis=0, keepdims=True)

    def delta_head(out, q, kk, v, s, gb, cs, hh):
        gc = cs[:, SM_A + hh:SM_A + hh + 1]
        beta = gb[:, SM_B + hh:SM_B + hh + 1]
        gr = row_of(gc)
        decay = jnp.where(tril, jnp.exp(jnp.where(tril, gc - gr, 0.0)), 0.0)
        eg = jnp.exp(gc)
        kb = kk * beta
        kq = _dot_nt(jnp.concatenate([kb, q], axis=0), kk)
        yield
        a = jnp.where(strict, kq[:L] * decay, 0.0)
        attn = kq[L:] * decay
        xs = jnp.concatenate([v * beta, kb * eg], axis=-1)
        pw = -a
        for it in range(n_double):
            upd = _dot(pw, xs)
            if it + 1 < n_double:
                pw = _dot(pw, pw)
            yield
            xs = xs + upd
        u = xs[:, :HEAD_DIM]
        w = xs[:, HEAD_DIM:]
        sb = s.astype(BF16)
        ws = _dot(w, sb)
        qs = _dot(q * eg, sb)
        yield
        v_new = u - ws
        g_last = cs[L - 1:L, SM_A + hh:SM_A + hh + 1]
        av = _dot(attn, v_new)
        kv = _dot_tn(kk * jnp.exp(g_last - gc), v_new)
        yield
        out["o"] = qs + av
        out["s"] = s * jnp.exp(g_last) + kv

    def mlstm_head(out, q, kk, v, cm, nrm, m, gb, cs, hh):
        bc = cs[:, SM_F + hh:SM_F + hh + 1]
        igc = gb[:, SM_I + hh:SM_I + hh + 1]
        br = row_of(bc)
        igr = row_of(igc)
        inter = bc + m
        logd = jnp.where(tril, bc - br + igr, NEG)
        m_t = jnp.maximum(inter, jnp.max(logd, axis=-1, keepdims=True))
        m_new = m_t[L - 1:L, :]
        b_last = bc[L - 1:L, :]
        kw = kk * jnp.exp(b_last - bc + igc - m_new)
        qk = _dot_nt(q, kk)
        qc = _dot(q, cm)
        kv = _dot_tn(kw, v)
        yield
        sc = qk * jnp.exp(logd - m_t)
        scv = _dot(sc, v)
        yield
        w_inter = jnp.exp(inter - m_t)
        num = w_inter * qc + scv
        den = w_inter * jnp.sum(q * nrm, axis=-1, keepdims=True) + jnp.sum(sc, axis=-1, keepdims=True)
        carry_f = jnp.exp(b_last + m - m_new)
        out["o"] = num / jnp.maximum(jnp.abs(den), jnp.exp(-m_t))
        out["c"] = carry_f * cm + kv
        out["n"] = carry_f * nrm + jnp.sum(kw, axis=0, keepdims=True)
        out["m"] = m_new

    def seq_body(i, carry):
        seqs = [i * SEQ_PER_ITER + d for d in range(SEQ_PER_ITER)]
        chains, results = [], []
        for b in seqs:
            rows = pl.ds(b, L, stride=NB)
            gb = gslab[rows, :]
            m_row = mlm_o[pl.ds(b, 1), :]
            cs = _cumsum_rows(tril_b, gb)
            dn_out = [dict() for _ in range(HEADS)]
            ml_out = [dict() for _ in range(HEADS)]
            for hh in range(HEADS):
                dn = [dnslab[c * HEADS + hh, rows, :] for c in range(3)]
                ml = [mlslab[c * HEADS + hh, rows, :] for c in range(3)]
                chains.append(delta_head(dn_out[hh], *dn, dns_o[b, hh], gb, cs, hh))
                chains.append(mlstm_head(ml_out[hh], *ml, mlc_o[b, hh], mln_o[b, pl.ds(hh, 1), :],
                                         m_row[:, hh:hh + 1], gb, cs, hh))
            results.append((dn_out, ml_out, m_row))
        while chains:
            alive = []
            for ch in chains:
                try:
                    next(ch)
                    alive.append(ch)
                except StopIteration:
                    pass
            chains = alive
        for b, (dn_out, ml_out, m_row) in zip(seqs, results):
            rows = pl.ds(b, L, stride=NB)
            for hh in range(HEADS):
                o_ref[hh, rows, :] = dn_out[hh]["o"]
                o_ref[HEADS + hh, rows, :] = ml_out[hh]["o"]
                dns_o[b, hh] = dn_out[hh]["s"]
                mlc_o[b, hh] = ml_out[hh]["c"]
                mln_o[b, pl.ds(hh, 1), :] = ml_out[hh]["n"]
                m_row = jnp.where(lane_row == hh, ml_out[hh]["m"], m_row)
            mlm_o[pl.ds(b, 1), :] = m_row
        return carry

    lax.fori_loop(0, NB // SEQ_PER_ITER, seq_body, 0)

    for j in range(S5_BLOCKS):
        cols = slice(j * S5_BCOLS, (j + 1) * S5_BCOLS)
        ucols = slice(j * S5_BIN, (j + 1) * S5_BIN)
        ub = u_scr[:, ucols].astype(BF16)
        hsre[...] = jnp.dot(ub, wbre_ref[j], preferred_element_type=F32)
        hsim[...] = jnp.dot(ub, wbim_ref[j], preferred_element_type=F32)
        are = jnp.broadcast_to(lamre_ref[:, cols], (NB, S5_BCOLS))
        aim = jnp.broadcast_to(lamim_ref[:, cols], (NB, S5_BCOLS))

        def scan_body(t, c, are=are, aim=aim):
            hr, hi = c
            rws = pl.ds(pl.multiple_of(t * NB, NB), NB)
            nr = are * hr - aim * hi + hsre[rws, :]
            ni = are * hi + aim * hr + hsim[rws, :]
            hsre[rws, :] = nr
            hsim[rws, :] = ni
            return nr, ni

        hr, hi = lax.fori_loop(0, L, scan_body, (s5re_o[:, cols], s5im_o[:, cols]))
        s5re_o[:, cols] = hr
        s5im_o[:, cols] = hi
        o_ref[2 * HEADS + j] = (_dot(hsre[...], wcre_ref[j]) - _dot(hsim[...], wcim_ref[j])
                                + s5d_ref[:, ucols] * u_scr[:, ucols])


def _const_spec(shape, index):
    return pl.BlockSpec(shape, lambda s, index=index: index, pipeline_mode=pl.Buffered(1))


def _mixer_call(layer, x, p, init, chunk, n_pad):
    n_steps, R, _ = x.shape
    has_init = init is not None
    n_state = (CONV_W - 1) * NB
    n_seq = n_steps * NB if has_init else NB
    n_blk = n_steps if has_init else 1
    sidx = (lambda s: s) if has_init else (lambda s: 0)

    in_specs = [
        pl.BlockSpec((None, R, D_MODEL), lambda s: (s, 0, 0)),
        _const_spec((None, 1, D_MODEL), (layer, 0, 0)),
        _const_spec((None, D_MODEL, A_COLS), (layer, 0, 0)),
        _const_spec((None, CONV_W, 3 * WIDTH), (layer, 0, 0)),
        _const_spec((None, 2, LANES), (layer, 0, 0)),
        _const_spec((None, S5_BLOCKS, S5_BIN, S5_BCOLS), (layer, 0, 0, 0)),
        _const_spec((None, S5_BLOCKS, S5_BIN, S5_BCOLS), (layer, 0, 0, 0)),
        _const_spec((None, S5_BLOCKS, S5_BCOLS, S5_BIN), (layer, 0, 0, 0)),
        _const_spec((None, S5_BLOCKS, S5_BCOLS, S5_BIN), (layer, 0, 0, 0)),
        _const_spec((None, 1, S5_COLS), (layer, 0, 0)),
        _const_spec((None, 1, S5_COLS), (layer, 0, 0)),
        _const_spec((None, 1, WIDTH), (layer, 0, 0)),
    ]
    args = [x, p["norm_w"], p["wa"], p["conv_w"], p["gpar"], p["wb_re"], p["wb_im"], p["wc_re"], p["wc_im"],
            p["lam_re"], p["lam_im"], p["s5_d"]]
    if has_init:
        in_specs += [
            pl.BlockSpec((None, None, n_state, 3 * WIDTH), lambda s: (layer, s, 0, 0)),
            pl.BlockSpec((None, NB, HEADS, HEAD_DIM, HEAD_DIM), lambda s: (layer, s, 0, 0, 0)),
            pl.BlockSpec((None, NB, HEADS, HEAD_DIM, HEAD_DIM), lambda s: (layer, s, 0, 0, 0)),
            pl.BlockSpec((None, NB, HEADS, HEAD_DIM), lambda s: (layer, s, 0, 0)),
            pl.BlockSpec((None, NB, LANES), lambda s: (layer, s, 0)),
            pl.BlockSpec((None, NB, S5_COLS), lambda s: (layer, s, 0)),
            pl.BlockSpec((None, NB, S5_COLS), lambda s: (layer, s, 0)),
        ]
        args += list(init)

    out_shape = (
        jax.ShapeDtypeStruct((n_steps, 3 * HEADS, R, LANES), F32),
        jax.ShapeDtypeStruct((n_blk, n_state, 3 * WIDTH), F32),
        jax.ShapeDtypeStruct((n_seq, HEADS, HEAD_DIM, HEAD_DIM), F32),
        jax.ShapeDtypeStruct((n_seq, HEADS, HEAD_DIM, HEAD_DIM), F32),
        jax.ShapeDtypeStruct((n_seq, HEADS, HEAD_DIM), F32),
        jax.ShapeDtypeStruct((n_seq, LANES), F32),
        jax.ShapeDtypeStruct((n_seq, S5_COLS), F32),
        jax.ShapeDtypeStruct((n_seq, S5_COLS), F32),
    )
    out_specs = (
        pl.BlockSpec((None, 3 * HEADS, R, LANES), lambda s: (s, 0, 0, 0)),
        pl.BlockSpec((None, n_state, 3 * WIDTH), lambda s: (sidx(s), 0, 0)),
        pl.BlockSpec((NB, HEADS, HEAD_DIM, HEAD_DIM), lambda s: (sidx(s), 0, 0, 0)),
        pl.BlockSpec((NB, HEADS, HEAD_DIM, HEAD_DIM), lambda s: (sidx(s), 0, 0, 0)),
        pl.BlockSpec((NB, HEADS, HEAD_DIM), lambda s: (sidx(s), 0, 0)),
        pl.BlockSpec((NB, LANES), lambda s: (sidx(s), 0)),
        pl.BlockSpec((NB, S5_COLS), lambda s: (sidx(s), 0)),
        pl.BlockSpec((NB, S5_COLS), lambda s: (sidx(s), 0)),
    )
    scratch = [
        pltpu.VMEM((n_state + R, 3 * WIDTH), F32),
        pltpu.VMEM((3 * HEADS, R, LANES), F32),
        pltpu.VMEM((3 * HEADS, R, LANES), F32),
        pltpu.VMEM((R, LANES), F32),
        pltpu.VMEM((R, WIDTH), F32),
        pltpu.VMEM((R, S5_BCOLS), F32),
        pltpu.VMEM((R, S5_BCOLS), F32),
    ]
    return pl.pallas_call(
        functools.partial(_mixer_kernel, chunk=chunk, n_pad=n_pad, has_init=has_init),
        grid=(n_steps,), in_specs=in_specs, out_specs=out_specs, out_shape=out_shape,
        scratch_shapes=scratch, name="mixer_sample" if has_init else "mixer_prompt",
        compiler_params=pltpu.CompilerParams(dimension_semantics=("arbitrary",),
                                             vmem_limit_bytes=VMEM_LIMIT_BYTES),
    )(*args)


def _merge_kernel(*refs, n_pad, final):
    (x_ref, o_ref, normw_ref, wb_ref, bgate_ref, dnnw_ref, mlnw_ref, wglu_ref, bglu_ref,
     wbr_ref, wout_ref) = refs[:11]
    fin_ref = refs[11] if final else None
    out_ref = refs[-1]
    sb, R, _ = x_ref.shape
    rt = sb * R
    x = x_ref[...].reshape(rt, D_MODEL)
    hb = _rms(x, normw_ref[...]).astype(BF16)

    def slab(i):
        return o_ref[:, i].reshape(rt, LANES)

    def head_cols(a, hh):
        return a[:, hh * HEAD_DIM:(hh + 1) * HEAD_DIM]

    zall = jnp.dot(hb, wb_ref[:, 0:B_GATE], preferred_element_type=F32)
    dn_z, ml_o = zall[:, 0:WIDTH], zall[:, WIDTH:2 * WIDTH]
    ml_z, s5_z = zall[:, 2 * WIDTH:3 * WIDTH], zall[:, 3 * WIDTH:4 * WIDTH]

    o_dn = jnp.concatenate(
        [_rms(slab(hh), dnnw_ref[...]) * _silu(head_cols(dn_z, hh)) for hh in range(HEADS)], axis=-1)
    o_ml = jnp.concatenate(
        [_rms(slab(HEADS + hh), mlnw_ref[...]) * _sigmoid(head_cols(ml_o, hh)) * _silu(head_cols(ml_z, hh))
         for hh in range(HEADS)], axis=-1)
    y5 = _gelu_tanh(jnp.concatenate([slab(2 * HEADS + j) for j in range(S5_BLOCKS)], axis=-1))
    y5 = y5 * _sigmoid(_dot(y5, wglu_ref[...]) + bglu_ref[...])
    o_s5 = y5 * _silu(s5_z)

    mixed = None
    for i, ob in enumerate((o_dn, o_ml, o_s5)):
        gcols = slice(i * D_MODEL, (i + 1) * D_MODEL)
        gate = _sigmoid(jnp.dot(hb, wb_ref[:, B_GATE + i * D_MODEL:B_GATE + (i + 1) * D_MODEL],
                                preferred_element_type=F32) + bgate_ref[:, gcols])
        term = gate * _dot(ob, wbr_ref[i])
        mixed = term if mixed is None else mixed + term
    delta = _dot(mixed, wout_ref[...])
    if n_pad:
        row = lax.broadcasted_iota(jnp.int32, (rt, 1), 0)
        delta = jnp.where(row >= jnp.where(pl.program_id(0) == 0, n_pad, 0), delta, 0.0)
    xo = x + delta
    if final:
        xo = _rms(xo, fin_ref[...])
    out_ref[...] = xo.reshape(sb, R, D_MODEL)


def _merge_call(layer, x, o_raw, p, final_w, sb, n_pad):
    n_steps, R, _ = x.shape
    final = final_w is not None
    in_specs = [
        pl.BlockSpec((sb, R, D_MODEL), lambda s: (s, 0, 0)),
        pl.BlockSpec((sb, 3 * HEADS, R, LANES), lambda s: (s, 0, 0, 0)),
        _const_spec((None, 1, D_MODEL), (layer, 0, 0)),
        _const_spec((None, D_MODEL, B_COLS), (layer, 0, 0)),
        _const_spec((None, 1, N_BRANCH * D_MODEL), (layer, 0, 0)),
        _const_spec((None, 1, HEAD_DIM), (layer, 0, 0)),
        _const_spec((None, 1, HEAD_DIM), (layer, 0, 0)),
        _const_spec((None, WIDTH, WIDTH), (layer, 0, 0)),
        _const_spec((None, 1, WIDTH), (layer, 0, 0)),
        _const_spec((None, N_BRANCH, WIDTH, D_MODEL), (layer, 0, 0, 0)),
        _const_spec((None, D_MODEL, D_MODEL), (layer, 0, 0)),
    ]
    args = [x, o_raw, p["norm_w"], p["wb"], p["b_gate"], p["dn_norm_w"], p["ml_norm_w"], p["w_glu"], p["b_glu"],
            p["w_branch"], p["w_out"]]
    if final:
        in_specs.append(_const_spec((1, D_MODEL), (0, 0)))
        args.append(final_w)
    return pl.pallas_call(
        functools.partial(_merge_kernel, n_pad=n_pad, final=final),
        grid=(n_steps // sb,), in_specs=in_specs,
        out_specs=pl.BlockSpec((sb, R, D_MODEL), lambda s: (s, 0, 0)),
        out_shape=jax.ShapeDtypeStruct(x.shape, F32), name="merge",
        compiler_params=pltpu.CompilerParams(dimension_semantics=("arbitrary",),
                                             vmem_limit_bytes=VMEM_LIMIT_BYTES),
    )(*args)


def _time_major(a, nb=NB):
    n, t = a.shape[0], a.shape[1]
    a = a.reshape((n // nb, nb, t) + a.shape[2:])
    a = jnp.swapaxes(a, 1, 2)
    return a.reshape((n // nb, t * nb) + a.shape[3:])


def _seq_major(a, t, nb=NB):
    g = a.shape[0]
    a = a.reshape((g, t, nb) + a.shape[2:])
    a = jnp.swapaxes(a, 1, 2)
    return a.reshape((g * nb, t) + a.shape[3:])


def kernel(x_prompt, x_sample, state_dn_conv, state_dn_s, state_ml_c, state_ml_n, state_ml_m, state_s5_re, state_s5_im, meta_tokens, norm_w, w_in, b_gate, dn_conv_w, dn_a_log, dn_dt_bias, dn_norm_w, ml_bias_i, ml_bias_f, ml_norm_w, s5_lambda_re, s5_lambda_im, s5_log_dt, s5_b_re, s5_b_im, s5_c_re, s5_c_im, s5_d, s5_w_glu, s5_b_glu, w_branch_dn, w_branch_ml, w_branch_s5, w_out, final_norm_w):
    depth = w_in.shape[0]
    bsz, seq, _ = x_prompt.shape
    dec_b, dec_t, _ = x_sample.shape
    assert bsz == NB and dec_b % NB == 0 and seq % CHUNK == 0 and dec_t >= CONV_W - 1

    cuts = [0]
    for sz in (3 * WIDTH, WIDTH, HEADS, HEADS, 3 * WIDTH, WIDTH, WIDTH, HEADS, HEADS, WIDTH, WIDTH, N_BRANCH * D_MODEL):
        cuts.append(cuts[-1] + sz)
    seg = lambda i: w_in[:, :, cuts[i]:cuts[i + 1]]
    (dn_qkv, dn_z, dn_a, dn_b, ml_qkv, ml_o, ml_z, ml_i, ml_f, s5_u, s5_z, gate_w) = [seg(i) for i in range(12)]
    small = jnp.concatenate([dn_a, dn_b, ml_i, ml_f, jnp.zeros((depth, D_MODEL, LANES - 4 * HEADS), F32)], axis=-1)
    wa = jnp.concatenate([dn_qkv, ml_qkv, s5_u, small], axis=-1).astype(BF16)
    wb = jnp.concatenate([dn_z, ml_o, ml_z, s5_z, gate_w], axis=-1).astype(BF16)

    def lanes_row(parts):
        row = jnp.zeros((depth, LANES), F32)
        for off, val in parts:
            row = row.at[:, off:off + HEADS].set(val)
        return row

    gpar = jnp.stack([lanes_row([(SM_A, dn_a_log)]),
                      lanes_row([(SM_A, dn_dt_bias), (SM_I, ml_bias_i), (SM_F, ml_bias_f)])], axis=1)
    lam_re, lam_im, wb_re, wb_im, wc_re, wc_im = _s5_prepare(
        s5_lambda_re, s5_lambda_im, s5_log_dt, s5_b_re, s5_b_im, s5_c_re, s5_c_im)
    p = dict(
        norm_w=norm_w[:, None, :], wa=wa, wb=wb, conv_w=dn_conv_w, gpar=gpar,
        wb_re=wb_re, wb_im=wb_im, wc_re=wc_re, wc_im=wc_im, lam_re=lam_re, lam_im=lam_im,
        s5_d=s5_d[:, None, :], b_gate=b_gate[:, None, :], dn_norm_w=dn_norm_w[:, None, :],
        ml_norm_w=ml_norm_w[:, None, :], w_glu=s5_w_glu.astype(BF16), b_glu=s5_b_glu[:, None, :],
        w_branch=jnp.stack([w_branch_dn, w_branch_ml, w_branch_s5], axis=1).astype(BF16),
        w_out=w_out.astype(BF16))
    fin = final_norm_w[None, :]

    n_front = CHUNK - N_META
    xp = jnp.concatenate([jnp.zeros((bsz, n_front, D_MODEL), F32),
                          jnp.broadcast_to(meta_tokens[None], (bsz, N_META, D_MODEL)), x_prompt], axis=1)
    t_all = xp.shape[1]
    xp = jnp.swapaxes(xp, 0, 1).reshape(t_all // CHUNK, CHUNK * NB, D_MODEL)
    xs = _time_major(x_sample)

    n_grp = dec_b // NB
    conv_in = jnp.swapaxes(state_dn_conv.reshape(depth, n_grp, NB, CONV_W - 1, 3 * WIDTH), 2, 3)
    conv_in = conv_in.reshape(depth, n_grp, (CONV_W - 1) * NB, 3 * WIDTH)
    mlm_in = jnp.pad(state_ml_m, ((0, 0), (0, 0), (0, LANES - HEADS)))
    init = (conv_in, state_dn_s, state_ml_c, state_ml_n, mlm_in,
            state_s5_re.reshape(depth, dec_b, S5_COLS), state_s5_im.reshape(depth, dec_b, S5_COLS))

    p_states, s_states = [], []
    for layer in range(depth):
        last = layer == depth - 1
        res = _mixer_call(layer, xp, p, None, CHUNK, n_front * NB)
        p_states.append(res[1:])
        xp = _merge_call(layer, xp, res[0], p, fin if last else None, 1, n_front * NB)
        res = _mixer_call(layer, xs, p, init, dec_t, 0)
        s_states.append(res[1:])
        xs = _merge_call(layer, xs, res[0], p, fin if last else None, n_grp, 0)

    y_prompt = jnp.swapaxes(xp.reshape(t_all, bsz, D_MODEL), 0, 1)[:, CHUNK:]
    y_sample = _seq_major(xs, dec_t)

    def collect(states):
        st = [jnp.stack([s[i] for s in states]) for i in range(7)]
        conv, dns, mlc, mln, mlm, s5r, s5i = st
        n_seq = dns.shape[1]
        conv = conv.reshape(depth, n_seq // NB, CONV_W - 1, NB, 3 * WIDTH)
        conv = jnp.swapaxes(conv, 2, 3).reshape(depth, n_seq, CONV_W - 1, 3 * WIDTH)
        return (conv, dns, mlc, mln, mlm[:, :, :HEADS],
                s5r.reshape(depth, n_seq, S5_GROUPS, S5_STATE), s5i.reshape(depth, n_seq, S5_GROUPS, S5_STATE))

    return (y_prompt, y_sample) + collect(p_states) + collect(s_states)
```

```python
import functools

import jax
import jax.numpy as jnp
from jax import lax
from jax.experimental import pallas as pl
from jax.experimental.pallas import tpu as pltpu

F32 = jnp.float32
BF16 = jnp.bfloat16

D_MODEL = 1024
N_META = 16
CHUNK = 64
HEADS = 4
HEAD_DIM = 128
WIDTH = HEADS * HEAD_DIM
CONV_W = 4
S5_GROUP = 16
S5_GROUPS = 32
S5_STATE = 64
S5_COLS = S5_GROUPS * S5_STATE
S5_BLOCKS = 4
S5_BCOLS = S5_COLS // S5_BLOCKS
S5_BIN = WIDTH // S5_BLOCKS
N_BRANCH = 3
NB = 8
SEQ_PER_ITER_LONG = 2
SEQ_PER_ITER_SHORT = 4
LANES = 128
EPS = 1e-6
NEG = -1e30
VMEM_LIMIT_BYTES = 56 * 1024 * 1024

A_DN, A_ML, A_S5, A_SM = 0, 3 * WIDTH, 6 * WIDTH, 7 * WIDTH
A_COLS = 7 * WIDTH + LANES
B_GATE = 4 * WIDTH
B_COLS = 4 * WIDTH + N_BRANCH * D_MODEL
SM_A, SM_B, SM_I, SM_F = 0, HEADS, 2 * HEADS, 3 * HEADS


def _dot(a, b):
    return jnp.dot(a.astype(BF16), b.astype(BF16), preferred_element_type=F32)


def _dot_wide_rhs(a, b):
    ah = a.astype(BF16)
    bh = b.astype(BF16)
    bl = (b - bh.astype(F32)).astype(BF16)
    d = functools.partial(jnp.dot, preferred_element_type=F32)
    return d(ah, bh) + d(ah, bl)


def _dot_nt(a, b):
    return lax.dot_general(a.astype(BF16), b.astype(BF16), (((1,), (1,)), ((), ())),
                           preferred_element_type=F32)


def _dot_tn(a, b):
    return lax.dot_general(a.astype(BF16), b.astype(BF16), (((0,), (0,)), ((), ())),
                           preferred_element_type=F32)


def _sigmoid(x):
    return 1.0 / (1.0 + jnp.exp(-x))


def _silu(x):
    return x * _sigmoid(x)


def _softplus(x):
    return jnp.maximum(x, 0.0) + jnp.log1p(jnp.exp(-jnp.abs(x)))


def _gelu_tanh(x):
    return 0.5 * x * (1.0 + jnp.tanh(0.7978845608028654 * (x + 0.044715 * (x * x * x))))


def _rms(x, w):
    return x * lax.rsqrt(jnp.mean(x * x, axis=-1, keepdims=True) + EPS) * w


def _cumsum_rows(tril_b, x):
    hi = x.astype(BF16)
    r1 = x - hi.astype(F32)
    mid = r1.astype(BF16)
    lo = (r1 - mid.astype(F32)).astype(BF16)
    d = functools.partial(jnp.dot, preferred_element_type=F32)
    return d(tril_b, hi) + d(tril_b, mid) + d(tril_b, lo)


def _s5_prep_kernel(lre_ref, lim_ref, ldt_ref, bre_ref, bim_ref, olre_ref, olim_ref, obre_ref, obim_ref):
    lre, lim = lre_ref[...], lim_ref[...]
    dt = jnp.exp(ldt_ref[...])
    mag = jnp.exp(lre * dt)
    lbr = mag * jnp.cos(lim * dt)
    lbi = mag * jnp.sin(lim * dt)
    nr, ni = lbr - 1.0, lbi
    den = lre * lre + lim * lim
    cr = (nr * lre + ni * lim) / den
    ci = (ni * lre - nr * lim) / den
    bre, bim = bre_ref[...], bim_ref[...]
    olre_ref[...] = lbr
    olim_ref[...] = lbi
    obre_ref[...] = cr * bre - ci * bim
    obim_ref[...] = cr * bim + ci * bre


def _s5_prepare(lam_re, lam_im, log_dt, b_re, b_im, c_re, c_im):
    depth = lam_re.shape[0]
    rows = depth * S5_GROUPS * S5_GROUP

    def expand(a):
        return jnp.broadcast_to(a[:, :, None, :], (depth, S5_GROUPS, S5_GROUP, S5_STATE)).reshape(rows, S5_STATE)

    ldt = jnp.broadcast_to(log_dt[:, :, None, None], (depth, S5_GROUPS, S5_GROUP, S5_STATE)).reshape(rows, S5_STATE)
    bt = lambda b: jnp.swapaxes(b, 2, 3).reshape(rows, S5_STATE)
    sds = jax.ShapeDtypeStruct((rows, S5_STATE), F32)
    lbr, lbi, bbr, bbi = pl.pallas_call(
        _s5_prep_kernel, out_shape=(sds, sds, sds, sds), name="s5_prep",
    )(expand(lam_re), expand(lam_im), ldt, bt(b_re), bt(b_im))

    lam_row = lambda a: a.reshape(depth, S5_GROUPS, S5_GROUP, S5_STATE)[:, :, 0, :].reshape(depth, 1, S5_COLS)
    gpb = S5_GROUPS // S5_BLOCKS
    eye = jnp.eye(gpb, dtype=F32)

    def in_map(bb):
        bb = bb.reshape(depth, S5_BLOCKS, gpb, S5_GROUP, S5_STATE)
        w = bb[:, :, :, :, None, :] * eye[None, None, :, None, :, None]
        return w.reshape(depth, S5_BLOCKS, S5_BIN, S5_BCOLS).astype(BF16)

    def out_map(c):
        c = jnp.swapaxes(c, 2, 3).reshape(depth, S5_BLOCKS, gpb, S5_STATE, S5_GROUP)
        w = c[:, :, :, :, None, :] * eye[None, None, :, None, :, None]
        return w.reshape(depth, S5_BLOCKS, S5_BCOLS, S5_BIN).astype(BF16)

    return lam_row(lbr), lam_row(lbi), in_map(bbr), in_map(bbi), out_map(c_re), out_map(c_im)


def _mixer_kernel(*refs, chunk, n_pad, has_init):
    L = chunk
    seq_per_iter = SEQ_PER_ITER_LONG if L >= CHUNK else SEQ_PER_ITER_SHORT
    R = L * NB
    (x_ref, normw_ref, wa_ref, convw_ref, gpar_ref, wbre_ref, wbim_ref, wcre_ref, wcim_ref,
     lamre_ref, lamim_ref, s5d_ref) = refs[:12]
    k = 12
    if has_init:
        conv_in, dns_in, mlc_in, mln_in, mlm_in, s5re_in, s5im_in = refs[k:k + 7]
        k += 7
    o_ref, conv_o, dns_o, mlc_o, mln_o, mlm_o, s5re_o, s5im_o = refs[k:k + 8]
    convbuf, dnslab, mlslab, gslab, u_scr, hsre, hsim = refs[k + 8:]
    step = pl.program_id(0)
    n_state = (CONV_W - 1) * NB

    if has_init:
        convbuf[0:n_state, :] = conv_in[...]
        dns_o[...] = dns_in[...]
        mlc_o[...] = mlc_in[...]
        mln_o[...] = mln_in[...]
        mlm_o[...] = mlm_in[...]
        s5re_o[...] = s5re_in[...]
        s5im_o[...] = s5im_in[...]
    else:
        @pl.when(step == 0)
        def _():
            convbuf[0:n_state, :] = jnp.zeros((n_state, 3 * WIDTH), F32)
            dns_o[...] = jnp.zeros(dns_o.shape, F32)
            mlc_o[...] = jnp.zeros(mlc_o.shape, F32)
            mln_o[...] = jnp.zeros(mln_o.shape, F32)
            mlm_o[...] = jnp.full(mlm_o.shape, NEG, F32)
            s5re_o[...] = jnp.zeros(s5re_o.shape, F32)
            s5im_o[...] = jnp.zeros(s5im_o.shape, F32)

    h = _rms(x_ref[...], normw_ref[...])
    if n_pad:
        row = lax.broadcasted_iota(jnp.int32, (R, 1), 0)
        valid = row >= jnp.where(step == 0, n_pad, 0)
        h = jnp.where(valid, h, 0.0)
    hb = h.astype(BF16)

    for c in range(3):
        convbuf[n_state:, c * WIDTH:(c + 1) * WIDTH] = jnp.dot(
            hb, wa_ref[:, A_DN + c * WIDTH:A_DN + (c + 1) * WIDTH], preferred_element_type=F32)
    for c in range(3):
        p = jnp.dot(hb, wa_ref[:, A_ML + c * WIDTH:A_ML + (c + 1) * WIDTH], preferred_element_type=F32)
        if c == 1:
            p = p * (HEAD_DIM ** -0.5)
        for hh in range(HEADS):
            mlslab[c * HEADS + hh] = p[:, hh * HEAD_DIM:(hh + 1) * HEAD_DIM]
    u_scr[...] = jnp.dot(hb, wa_ref[:, A_S5:A_S5 + WIDTH], preferred_element_type=F32)

    z = jnp.dot(hb, wa_ref[:, A_SM:A_SM + LANES], preferred_element_type=F32) + gpar_ref[1:2, :]
    lane = lax.broadcasted_iota(jnp.int32, (R, LANES), 1)
    ig = z
    if n_pad:
        ig = jnp.where(valid, ig, NEG)
    gates = jnp.where(lane < SM_B, -jnp.exp(gpar_ref[0:1, :]) * _softplus(z),
                      jnp.where(lane < SM_I, _sigmoid(z),
                                jnp.where(lane < SM_F, ig,
                                          jnp.where(lane < SM_F + HEADS, -_softplus(-z), 0.0))))
    gslab[...] = gates

    for c in range(3):
        for hh in range(HEADS):
            col = c * WIDTH + hh * HEAD_DIM
            acc = convw_ref[0:1, col:col + HEAD_DIM] * convbuf[0:R, col:col + HEAD_DIM]
            for j in range(1, CONV_W):
                acc = acc + convw_ref[j:j + 1, col:col + HEAD_DIM] * convbuf[j * NB:j * NB + R, col:col + HEAD_DIM]
            y = _silu(acc)
            if c < 2:
                y = y * lax.rsqrt(jnp.sum(y * y, axis=-1, keepdims=True) + EPS)
            if c == 0:
                y = y * (HEAD_DIM ** -0.5)
            dnslab[c * HEADS + hh] = y
    new_conv = convbuf[R:R + n_state, :]
    conv_o[...] = new_conv
    convbuf[0:n_state, :] = new_conv

    ii = lax.broadcasted_iota(jnp.int32, (L, L), 0)
    jj = lax.broadcasted_iota(jnp.int32, (L, L), 1)
    tril = ii >= jj
    strict = ii > jj
    eye = ii == jj
    tril_b = jnp.where(tril, 1.0, 0.0).astype(BF16)
    lane_row = lax.broadcasted_iota(jnp.int32, (1, LANES), 1)
    n_double = max(1, (L - 1).bit_length())

    def row_of(col):
        return jnp.sum(jnp.where(eye, col, 0.0), axis=0, keepdims=True)

    def delta_head(out, q, kk, v, s, gb, cs, hh):
        gc = cs[:, SM_A + hh:SM_A + hh + 1]
        beta = gb[:, SM_B + hh:SM_B + hh + 1]
        gr = row_of(gc)
        decay = jnp.where(tril, jnp.exp(jnp.where(tril, gc - gr, 0.0)), 0.0)
        eg = jnp.exp(gc)
        kb = kk * beta
        kq = _dot_nt(jnp.concatenate([kb, q], axis=0), kk)
        yield
        a = jnp.where(strict, kq[:L] * decay, 0.0)
        attn = kq[L:] * decay
        xs = jnp.concatenate([v * beta, kb * eg], axis=-1)
        pw = -a
        for it in range(n_double):
            upd = _dot_wide_rhs(pw, xs)
            if it + 1 < n_double:
                pw = _dot(pw, pw)
            yield
            xs = xs + upd
        u = xs[:, :HEAD_DIM]
        w = xs[:, HEAD_DIM:]
        sb = s.astype(BF16)
        ws = _dot(w, sb)
        qs = _dot(q * eg, sb)
        yield
        v_new = u - ws
        g_last = cs[L - 1:L, SM_A + hh:SM_A + hh + 1]
        av = _dot(attn, v_new)
        kv = _dot_tn(kk * jnp.exp(g_last - gc), v_new)
        yield
        out["o"] = qs + av
        out["s"] = s * jnp.exp(g_last) + kv

    def mlstm_head(out, q, kk, v, cm, nrm, m, gb, cs, hh):
        bc = cs[:, SM_F + hh:SM_F + hh + 1]
        igc = gb[:, SM_I + hh:SM_I + hh + 1]
        br = row_of(bc)
        igr = row_of(igc)
        inter = bc + m
        logd = jnp.where(tril, bc - br + igr, NEG)
        m_t = jnp.maximum(inter, jnp.max(logd, axis=-1, keepdims=True))
        m_new = m_t[L - 1:L, :]
        b_last = bc[L - 1:L, :]
        kw = kk * jnp.exp(b_last - bc + igc - m_new)
        qk = _dot_nt(q, kk)
        qc = _dot(q, cm)
        kv = _dot_tn(kw, v)
        yield
        sc = qk * jnp.exp(logd - m_t)
        scv = _dot(sc, v)
        yield
        w_inter = jnp.exp(inter - m_t)
        num = w_inter * qc + scv
        den = w_inter * jnp.sum(q * nrm, axis=-1, keepdims=True) + jnp.sum(sc, axis=-1, keepdims=True)
        carry_f = jnp.exp(b_last + m - m_new)
        out["o"] = num / jnp.maximum(jnp.abs(den), jnp.exp(-m_t))
        out["c"] = carry_f * cm + kv
        out["n"] = carry_f * nrm + jnp.sum(kw, axis=0, keepdims=True)
        out["m"] = m_new

    def seq_body(i, carry):
        seqs = [i * seq_per_iter + d for d in range(seq_per_iter)]
        chains, results = [], []
        for b in seqs:
            rows = pl.ds(b, L, stride=NB)
            gb = gslab[rows, :]
            m_row = mlm_o[pl.ds(b, 1), :]
            cs = _cumsum_rows(tril_b, gb)
            dn_out = [dict() for _ in range(HEADS)]
            ml_out = [dict() for _ in range(HEADS)]
            for hh in range(HEADS):
                dn = [dnslab[c * HEADS + hh, rows, :] for c in range(3)]
                ml = [mlslab[c * HEADS + hh, rows, :] for c in range(3)]
                chains.append(delta_head(dn_out[hh], *dn, dns_o[b, hh], gb, cs, hh))
                chains.append(mlstm_head(ml_out[hh], *ml, mlc_o[b, hh], mln_o[b, pl.ds(hh, 1), :],
                                         m_row[:, hh:hh + 1], gb, cs, hh))
            results.append((dn_out, ml_out, m_row))
        while chains:
            alive = []
            for ch in chains:
                try:
                    next(ch)
                    alive.append(ch)
                except StopIteration:
                    pass
            chains = alive
        for b, (dn_out, ml_out, m_row) in zip(seqs, results):
            rows = pl.ds(b, L, stride=NB)
            for hh in range(HEADS):
                o_ref[hh, rows, :] = dn_out[hh]["o"]
                o_ref[HEADS + hh, rows, :] = ml_out[hh]["o"]
                dns_o[b, hh] = dn_out[hh]["s"]
                mlc_o[b, hh] = ml_out[hh]["c"]
                mln_o[b, pl.ds(hh, 1), :] = ml_out[hh]["n"]
                m_row = jnp.where(lane_row == hh, ml_out[hh]["m"], m_row)
            mlm_o[pl.ds(b, 1), :] = m_row
        return carry

    lax.fori_loop(0, NB // seq_per_iter, seq_body, 0)

    for j in range(S5_BLOCKS):
        cols = slice(j * S5_BCOLS, (j + 1) * S5_BCOLS)
        ucols = slice(j * S5_BIN, (j + 1) * S5_BIN)
        ub = u_scr[:, ucols].astype(BF16)
        hsre[...] = jnp.dot(ub, wbre_ref[j], preferred_element_type=F32)
        hsim[...] = jnp.dot(ub, wbim_ref[j], preferred_element_type=F32)
        are = jnp.broadcast_to(lamre_ref[:, cols], (NB, S5_BCOLS))
        aim = jnp.broadcast_to(lamim_ref[:, cols], (NB, S5_BCOLS))

        def scan_body(t, c, are=are, aim=aim):
            hr, hi = c
            rws = pl.ds(pl.multiple_of(t * NB, NB), NB)
            nr = are * hr - aim * hi + hsre[rws, :]
            ni = are * hi + aim * hr + hsim[rws, :]
            hsre[rws, :] = nr
            hsim[rws, :] = ni
            return nr, ni

        hr, hi = lax.fori_loop(0, L, scan_body, (s5re_o[:, cols], s5im_o[:, cols]))
        s5re_o[:, cols] = hr
        s5im_o[:, cols] = hi
        o_ref[2 * HEADS + j] = (_dot(hsre[...], wcre_ref[j]) - _dot(hsim[...], wcim_ref[j])
                                + s5d_ref[:, ucols] * u_scr[:, ucols])


def _const_spec(shape, index):
    return pl.BlockSpec(shape, lambda s, index=index: index, pipeline_mode=pl.Buffered(1))


def _mixer_call(layer, x, p, init, chunk, n_pad):
    n_steps, R, _ = x.shape
    has_init = init is not None
    n_state = (CONV_W - 1) * NB
    n_seq = n_steps * NB if has_init else NB
    n_blk = n_steps if has_init else 1
    sidx = (lambda s: s) if has_init else (lambda s: 0)

    in_specs = [
        pl.BlockSpec((None, R, D_MODEL), lambda s: (s, 0, 0)),
        _const_spec((None, 1, D_MODEL), (layer, 0, 0)),
        _const_spec((None, D_MODEL, A_COLS), (layer, 0, 0)),
        _const_spec((None, CONV_W, 3 * WIDTH), (layer, 0, 0)),
        _const_spec((None, 2, LANES), (layer, 0, 0)),
        _const_spec((None, S5_BLOCKS, S5_BIN, S5_BCOLS), (layer, 0, 0, 0)),
        _const_spec((None, S5_BLOCKS, S5_BIN, S5_BCOLS), (layer, 0, 0, 0)),
        _const_spec((None, S5_BLOCKS, S5_BCOLS, S5_BIN), (layer, 0, 0, 0)),
        _const_spec((None, S5_BLOCKS, S5_BCOLS, S5_BIN), (layer, 0, 0, 0)),
        _const_spec((None, 1, S5_COLS), (layer, 0, 0)),
        _const_spec((None, 1, S5_COLS), (layer, 0, 0)),
        _const_spec((None, 1, WIDTH), (layer, 0, 0)),
    ]
    args = [x, p["norm_w"], p["wa"], p["conv_w"], p["gpar"], p["wb_re"], p["wb_im"], p["wc_re"], p["wc_im"],
            p["lam_re"], p["lam_im"], p["s5_d"]]
    if has_init:
        in_specs += [
            pl.BlockSpec((None, None, n_state, 3 * WIDTH), lambda s: (layer, s, 0, 0)),
            pl.BlockSpec((None, NB, HEADS, HEAD_DIM, HEAD_DIM), lambda s: (layer, s, 0, 0, 0)),
            pl.BlockSpec((None, NB, HEADS, HEAD_DIM, HEAD_DIM), lambda s: (layer, s, 0, 0, 0)),
            pl.BlockSpec((None, NB, HEADS, HEAD_DIM), lambda s: (layer, s, 0, 0)),
            pl.BlockSpec((None, NB, LANES), lambda s: (layer, s, 0)),
            pl.BlockSpec((None, NB, S5_COLS), lambda s: (layer, s, 0)),
            pl.BlockSpec((None, NB, S5_COLS), lambda s: (layer, s, 0)),
        ]
        args += list(init)

    out_shape = (
        jax.ShapeDtypeStruct((n_steps, 3 * HEADS, R, LANES), F32),
        jax.ShapeDtypeStruct((n_blk, n_state, 3 * WIDTH), F32),
        jax.ShapeDtypeStruct((n_seq, HEADS, HEAD_DIM, HEAD_DIM), F32),
        jax.ShapeDtypeStruct((n_seq, HEADS, HEAD_DIM, HEAD_DIM), F32),
        jax.ShapeDtypeStruct((n_seq, HEADS, HEAD_DIM), F32),
        jax.ShapeDtypeStruct((n_seq, LANES), F32),
        jax.ShapeDtypeStruct((n_seq, S5_COLS), F32),
        jax.ShapeDtypeStruct((n_seq, S5_COLS), F32),
    )
    out_specs = (
        pl.BlockSpec((None, 3 * HEADS, R, LANES), lambda s: (s, 0, 0, 0)),
        pl.BlockSpec((None, n_state, 3 * WIDTH), lambda s: (sidx(s), 0, 0)),
        pl.BlockSpec((NB, HEADS, HEAD_DIM, HEAD_DIM), lambda s: (sidx(s), 0, 0, 0)),
        pl.BlockSpec((NB, HEADS, HEAD_DIM, HEAD_DIM), lambda s: (sidx(s), 0, 0, 0)),
        pl.BlockSpec((NB, HEADS, HEAD_DIM), lambda s: (sidx(s), 0, 0)),
        pl.BlockSpec((NB, LANES), lambda s: (sidx(s), 0)),
        pl.BlockSpec((NB, S5_COLS), lambda s: (sidx(s), 0)),
        pl.BlockSpec((NB, S5_COLS), lambda s: (sidx(s), 0)),
    )
    scratch = [
        pltpu.VMEM((n_state + R, 3 * WIDTH), F32),
        pltpu.VMEM((3 * HEADS, R, LANES), F32),
        pltpu.VMEM((3 * HEADS, R, LANES), F32),
        pltpu.VMEM((R, LANES), F32),
        pltpu.VMEM((R, WIDTH), F32),
        pltpu.VMEM((R, S5_BCOLS), F32),
        pltpu.VMEM((R, S5_BCOLS), F32),
    ]
    return pl.pallas_call(
        functools.partial(_mixer_kernel, chunk=chunk, n_pad=n_pad, has_init=has_init),
        grid=(n_steps,), in_specs=in_specs, out_specs=out_specs, out_shape=out_shape,
        scratch_shapes=scratch, name="mixer_sample" if has_init else "mixer_prompt",
        compiler_params=pltpu.CompilerParams(dimension_semantics=("arbitrary",),
                                             vmem_limit_bytes=VMEM_LIMIT_BYTES),
    )(*args)


def _merge_kernel(*refs, n_pad, final):
    (x_ref, o_ref, normw_ref, wb_ref, bgate_ref, dnnw_ref, mlnw_ref, wglu_ref, bglu_ref,
     wbr_ref, wout_ref) = refs[:11]
    fin_ref = refs[11] if final else None
    out_ref = refs[-1]
    sb, R, _ = x_ref.shape
    rt = sb * R
    x = x_ref[...].reshape(rt, D_MODEL)
    hb = _rms(x, normw_ref[...]).astype(BF16)

    def slab(i):
        return o_ref[:, i].reshape(rt, LANES)

    def head_cols(a, hh):
        return a[:, hh * HEAD_DIM:(hh + 1) * HEAD_DIM]

    zall = jnp.dot(hb, wb_ref[:, 0:B_GATE], preferred_element_type=F32)
    dn_z, ml_o = zall[:, 0:WIDTH], zall[:, WIDTH:2 * WIDTH]
    ml_z, s5_z = zall[:, 2 * WIDTH:3 * WIDTH], zall[:, 3 * WIDTH:4 * WIDTH]

    o_dn = jnp.concatenate(
        [_rms(slab(hh), dnnw_ref[...]) * _silu(head_cols(dn_z, hh)) for hh in range(HEADS)], axis=-1)
    o_ml = jnp.concatenate(
        [_rms(slab(HEADS + hh), mlnw_ref[...]) * _sigmoid(head_cols(ml_o, hh)) * _silu(head_cols(ml_z, hh))
         for hh in range(HEADS)], axis=-1)
    y5 = _gelu_tanh(jnp.concatenate([slab(2 * HEADS + j) for j in range(S5_BLOCKS)], axis=-1))
    y5 = y5 * _sigmoid(_dot(y5, wglu_ref[...]) + bglu_ref[...])
    o_s5 = y5 * _silu(s5_z)

    mixed = None
    for i, ob in enumerate((o_dn, o_ml, o_s5)):
        gcols = slice(i * D_MODEL, (i + 1) * D_MODEL)
        gate = _sigmoid(jnp.dot(hb, wb_ref[:, B_GATE + i * D_MODEL:B_GATE + (i + 1) * D_MODEL],
                                preferred_element_type=F32) + bgate_ref[:, gcols])
        term = gate * _dot(ob, wbr_ref[i])
        mixed = term if mixed is None else mixed + term
    delta = _dot(mixed, wout_ref[...])
    if n_pad:
        row = lax.broadcasted_iota(jnp.int32, (rt, 1), 0)
        delta = jnp.where(row >= jnp.where(pl.program_id(0) == 0, n_pad, 0), delta, 0.0)
    xo = x + delta
    if final:
        xo = _rms(xo, fin_ref[...])
    out_ref[...] = xo.reshape(sb, R, D_MODEL)


def _merge_call(layer, x, o_raw, p, final_w, sb, n_pad):
    n_steps, R, _ = x.shape
    final = final_w is not None
    in_specs = [
        pl.BlockSpec((sb, R, D_MODEL), lambda s: (s, 0, 0)),
        pl.BlockSpec((sb, 3 * HEADS, R, LANES), lambda s: (s, 0, 0, 0)),
        _const_spec((None, 1, D_MODEL), (layer, 0, 0)),
        _const_spec((None, D_MODEL, B_COLS), (layer, 0, 0)),
        _const_spec((None, 1, N_BRANCH * D_MODEL), (layer, 0, 0)),
        _const_spec((None, 1, HEAD_DIM), (layer, 0, 0)),
        _const_spec((None, 1, HEAD_DIM), (layer, 0, 0)),
        _const_spec((None, WIDTH, WIDTH), (layer, 0, 0)),
        _const_spec((None, 1, WIDTH), (layer, 0, 0)),
        _const_spec((None, N_BRANCH, WIDTH, D_MODEL), (layer, 0, 0, 0)),
        _const_spec((None, D_MODEL, D_MODEL), (layer, 0, 0)),
    ]
    args = [x, o_raw, p["norm_w"], p["wb"], p["b_gate"], p["dn_norm_w"], p["ml_norm_w"], p["w_glu"], p["b_glu"],
            p["w_branch"], p["w_out"]]
    if final:
        in_specs.append(_const_spec((1, D_MODEL), (0, 0)))
        args.append(final_w)
    return pl.pallas_call(
        functools.partial(_merge_kernel, n_pad=n_pad, final=final),
        grid=(n_steps // sb,), in_specs=in_specs,
        out_specs=pl.BlockSpec((sb, R, D_MODEL), lambda s: (s, 0, 0)),
        out_shape=jax.ShapeDtypeStruct(x.shape, F32), name="merge",
        compiler_params=pltpu.CompilerParams(dimension_semantics=("arbitrary",),
                                             vmem_limit_bytes=VMEM_LIMIT_BYTES),
    )(*args)


def _time_major(a, nb=NB):
    n, t = a.shape[0], a.shape[1]
    a = a.reshape((n // nb, nb, t) + a.shape[2:])
    a = jnp.swapaxes(a, 1, 2)
    return a.reshape((n // nb, t * nb) + a.shape[3:])


def _seq_major(a, t, nb=NB):
    g = a.shape[0]
    a = a.reshape((g, t, nb) + a.shape[2:])
    a = jnp.swapaxes(a, 1, 2)
    return a.reshape((g * nb, t) + a.shape[3:])


def kernel(x_prompt, x_sample, state_dn_conv, state_dn_s, state_ml_c, state_ml_n, state_ml_m, state_s5_re, state_s5_im, meta_tokens, norm_w, w_in, b_gate, dn_conv_w, dn_a_log, dn_dt_bias, dn_norm_w, ml_bias_i, ml_bias_f, ml_norm_w, s5_lambda_re, s5_lambda_im, s5_log_dt, s5_b_re, s5_b_im, s5_c_re, s5_c_im, s5_d, s5_w_glu, s5_b_glu, w_branch_dn, w_branch_ml, w_branch_s5, w_out, final_norm_w):
    depth = w_in.shape[0]
    bsz, seq, _ = x_prompt.shape
    dec_b, dec_t, _ = x_sample.shape
    assert bsz == NB and dec_b % NB == 0 and seq % CHUNK == 0 and dec_t >= CONV_W - 1

    cuts = [0]
    for sz in (3 * WIDTH, WIDTH, HEADS, HEADS, 3 * WIDTH, WIDTH, WIDTH, HEADS, HEADS, WIDTH, WIDTH, N_BRANCH * D_MODEL):
        cuts.append(cuts[-1] + sz)
    seg = lambda i: w_in[:, :, cuts[i]:cuts[i + 1]]
    (dn_qkv, dn_z, dn_a, dn_b, ml_qkv, ml_o, ml_z, ml_i, ml_f, s5_u, s5_z, gate_w) = [seg(i) for i in range(12)]
    small = jnp.concatenate([dn_a, dn_b, ml_i, ml_f, jnp.zeros((depth, D_MODEL, LANES - 4 * HEADS), F32)], axis=-1)
    wa = jnp.concatenate([dn_qkv, ml_qkv, s5_u, small], axis=-1).astype(BF16)
    wb = jnp.concatenate([dn_z, ml_o, ml_z, s5_z, gate_w], axis=-1).astype(BF16)

    def lanes_row(parts):
        row = jnp.zeros((depth, LANES), F32)
        for off, val in parts:
            row = row.at[:, off:off + HEADS].set(val)
        return row

    gpar = jnp.stack([lanes_row([(SM_A, dn_a_log)]),
                      lanes_row([(SM_A, dn_dt_bias), (SM_I, ml_bias_i), (SM_F, ml_bias_f)])], axis=1)
    lam_re, lam_im, wb_re, wb_im, wc_re, wc_im = _s5_prepare(
        s5_lambda_re, s5_lambda_im, s5_log_dt, s5_b_re, s5_b_im, s5_c_re, s5_c_im)
    p = dict(
        norm_w=norm_w[:, None, :], wa=wa, wb=wb, conv_w=dn_conv_w, gpar=gpar,
        wb_re=wb_re, wb_im=wb_im, wc_re=wc_re, wc_im=wc_im, lam_re=lam_re, lam_im=lam_im,
        s5_d=s5_d[:, None, :], b_gate=b_gate[:, None, :], dn_norm_w=dn_norm_w[:, None, :],
        ml_norm_w=ml_norm_w[:, None, :], w_glu=s5_w_glu.astype(BF16), b_glu=s5_b_glu[:, None, :],
        w_branch=jnp.stack([w_branch_dn, w_branch_ml, w_branch_s5], axis=1).astype(BF16),
        w_out=w_out.astype(BF16))
    fin = final_norm_w[None, :]

    n_front = CHUNK - N_META
    xp = jnp.concatenate([jnp.zeros((bsz, n_front, D_MODEL), F32),
                          jnp.broadcast_to(meta_tokens[None], (bsz, N_META, D_MODEL)), x_prompt], axis=1)
    t_all = xp.shape[1]
    xp = jnp.swapaxes(xp, 0, 1).reshape(t_all // CHUNK, CHUNK * NB, D_MODEL)
    xs = _time_major(x_sample)

    n_grp = dec_b // NB
    conv_in = jnp.swapaxes(state_dn_conv.reshape(depth, n_grp, NB, CONV_W - 1, 3 * WIDTH), 2, 3)
    conv_in = conv_in.reshape(depth, n_grp, (CONV_W - 1) * NB, 3 * WIDTH)
    mlm_in = jnp.pad(state_ml_m, ((0, 0), (0, 0), (0, LANES - HEADS)))
    init = (conv_in, state_dn_s, state_ml_c, state_ml_n, mlm_in,
            state_s5_re.reshape(depth, dec_b, S5_COLS), state_s5_im.reshape(depth, dec_b, S5_COLS))

    p_states, s_states = [], []
    for layer in range(depth):
        last = layer == depth - 1
        res = _mixer_call(layer, xp, p, None, CHUNK, n_front * NB)
        p_states.append(res[1:])
        xp = _merge_call(layer, xp, res[0], p, fin if last else None, 1, n_front * NB)
        res = _mixer_call(layer, xs, p, init, dec_t, 0)
        s_states.append(res[1:])
        xs = _merge_call(layer, xs, res[0], p, fin if last else None, n_grp, 0)

    y_prompt = jnp.swapaxes(xp.reshape(t_all, bsz, D_MODEL), 0, 1)[:, CHUNK:]
    y_sample = _seq_major(xs, dec_t)

    def collect(states):
        st = [jnp.stack([s[i] for s in states]) for i in range(7)]
        conv, dns, mlc, mln, mlm, s5r, s5i = st
        n_seq = dns.shape[1]
        conv = conv.reshape(depth, n_seq // NB, CONV_W - 1, NB, 3 * WIDTH)
        conv = jnp.swapaxes(conv, 2, 3).reshape(depth, n_seq, CONV_W - 1, 3 * WIDTH)
        return (conv, dns, mlc, mln, mlm[:, :, :HEADS],
                s5r.reshape(depth, n_seq, S5_GROUPS, S5_STATE), s5i.reshape(depth, n_seq, S5_GROUPS, S5_STATE))

    return (y_prompt, y_sample) + collect(p_states) + collect(s_states)
```

```python
import functools

import jax
import jax.numpy as jnp
from jax import lax
from jax.experimental import pallas as pl
from jax.experimental.pallas import tpu as pltpu

F32 = jnp.float32
BF16 = jnp.bfloat16

D_MODEL = 1024
N_META = 16
CHUNK = 64
HEADS = 4
HEAD_DIM = 128
WIDTH = HEADS * HEAD_DIM
CONV_W = 4
S5_GROUP = 16
S5_GROUPS = 32
S5_STATE = 64
S5_COLS = S5_GROUPS * S5_STATE
S5_BLOCKS = 4
S5_BCOLS = S5_COLS // S5_BLOCKS
S5_BIN = WIDTH // S5_BLOCKS
N_BRANCH = 3
NB = 8
WIDE_ROUNDS = 3
S5_SCAN_UNROLL = 4
SEQ_PER_ITER_LONG = 2
SEQ_PER_ITER_SHORT = 4
LANES = 128
EPS = 1e-6
NEG = -1e30
VMEM_LIMIT_BYTES = 56 * 1024 * 1024

A_DN, A_ML, A_S5, A_SM = 0, 3 * WIDTH, 6 * WIDTH, 7 * WIDTH
A_COLS = 7 * WIDTH + LANES
B_GATE = 4 * WIDTH
B_COLS = 4 * WIDTH + N_BRANCH * D_MODEL
SM_A, SM_B, SM_I, SM_F = 0, HEADS, 2 * HEADS, 3 * HEADS


def _dot(a, b):
    return jnp.dot(a.astype(BF16), b.astype(BF16), preferred_element_type=F32)


def _dot_wide_rhs(a, b):
    ah = a.astype(BF16)
    bh = b.astype(BF16)
    bl = (b - bh.astype(F32)).astype(BF16)
    d = functools.partial(jnp.dot, preferred_element_type=F32)
    return d(ah, bh) + d(ah, bl)


def _dot_nt(a, b):
    return lax.dot_general(a.astype(BF16), b.astype(BF16), (((1,), (1,)), ((), ())),
                           preferred_element_type=F32)


def _dot_tn(a, b):
    return lax.dot_general(a.astype(BF16), b.astype(BF16), (((0,), (0,)), ((), ())),
                           preferred_element_type=F32)


def _sigmoid(x):
    return 1.0 / (1.0 + jnp.exp(-x))


def _silu(x):
    return x * _sigmoid(x)


def _softplus(x):
    return jnp.maximum(x, 0.0) + jnp.log1p(jnp.exp(-jnp.abs(x)))


def _gelu_tanh(x):
    return 0.5 * x * (1.0 + jnp.tanh(0.7978845608028654 * (x + 0.044715 * (x * x * x))))


def _rms(x, w):
    return x * lax.rsqrt(jnp.mean(x * x, axis=-1, keepdims=True) + EPS) * w


def _cumsum_rows(tril_b, x):
    hi = x.astype(BF16)
    r1 = x - hi.astype(F32)
    mid = r1.astype(BF16)
    lo = (r1 - mid.astype(F32)).astype(BF16)
    d = functools.partial(jnp.dot, preferred_element_type=F32)
    return d(tril_b, hi) + d(tril_b, mid) + d(tril_b, lo)


def _s5_prep_kernel(lre_ref, lim_ref, ldt_ref, bre_ref, bim_ref, olre_ref, olim_ref, obre_ref, obim_ref):
    lre, lim = lre_ref[...], lim_ref[...]
    dt = jnp.exp(ldt_ref[...])
    mag = jnp.exp(lre * dt)
    lbr = mag * jnp.cos(lim * dt)
    lbi = mag * jnp.sin(lim * dt)
    nr, ni = lbr - 1.0, lbi
    den = lre * lre + lim * lim
    cr = (nr * lre + ni * lim) / den
    ci = (ni * lre - nr * lim) / den
    bre, bim = bre_ref[...], bim_ref[...]
    olre_ref[...] = lbr
    olim_ref[...] = lbi
    obre_ref[...] = cr * bre - ci * bim
    obim_ref[...] = cr * bim + ci * bre


def _s5_prepare(lam_re, lam_im, log_dt, b_re, b_im, c_re, c_im):
    depth = lam_re.shape[0]
    rows = depth * S5_GROUPS * S5_GROUP

    def expand(a):
        return jnp.broadcast_to(a[:, :, None, :], (depth, S5_GROUPS, S5_GROUP, S5_STATE)).reshape(rows, S5_STATE)

    ldt = jnp.broadcast_to(log_dt[:, :, None, None], (depth, S5_GROUPS, S5_GROUP, S5_STATE)).reshape(rows, S5_STATE)
    bt = lambda b: jnp.swapaxes(b, 2, 3).reshape(rows, S5_STATE)
    sds = jax.ShapeDtypeStruct((rows, S5_STATE), F32)
    lbr, lbi, bbr, bbi = pl.pallas_call(
        _s5_prep_kernel, out_shape=(sds, sds, sds, sds), name="s5_prep",
    )(expand(lam_re), expand(lam_im), ldt, bt(b_re), bt(b_im))

    lam_row = lambda a: a.reshape(depth, S5_GROUPS, S5_GROUP, S5_STATE)[:, :, 0, :].reshape(depth, 1, S5_COLS)
    gpb = S5_GROUPS // S5_BLOCKS
    eye = jnp.eye(gpb, dtype=F32)

    def in_map(bb):
        bb = bb.reshape(depth, S5_BLOCKS, gpb, S5_GROUP, S5_STATE)
        w = bb[:, :, :, :, None, :] * eye[None, None, :, None, :, None]
        return w.reshape(depth, S5_BLOCKS, S5_BIN, S5_BCOLS).astype(BF16)

    def out_map(c):
        c = jnp.swapaxes(c, 2, 3).reshape(depth, S5_BLOCKS, gpb, S5_STATE, S5_GROUP)
        w = c[:, :, :, :, None, :] * eye[None, None, :, None, :, None]
        return w.reshape(depth, S5_BLOCKS, S5_BCOLS, S5_BIN).astype(BF16)

    return lam_row(lbr), lam_row(lbi), in_map(bbr), in_map(bbi), out_map(c_re), out_map(c_im)


def _mixer_kernel(*refs, chunk, n_pad, has_init):
    L = chunk
    seq_per_iter = SEQ_PER_ITER_LONG if L >= CHUNK else SEQ_PER_ITER_SHORT
    R = L * NB
    (x_ref, normw_ref, wa_ref, convw_ref, gpar_ref, wbre_ref, wbim_ref, wcre_ref, wcim_ref,
     lamre_ref, lamim_ref, s5d_ref) = refs[:12]
    k = 12
    if has_init:
        conv_in, dns_in, mlc_in, mln_in, mlm_in, s5re_in, s5im_in = refs[k:k + 7]
        k += 7
    o_ref, conv_o, dns_o, mlc_o, mln_o, mlm_o, s5re_o, s5im_o = refs[k:k + 8]
    convbuf, dnslab, mlslab, gslab, u_scr, hsre, hsim = refs[k + 8:]
    step = pl.program_id(0)
    n_state = (CONV_W - 1) * NB

    if has_init:
        convbuf[0:n_state, :] = conv_in[...]
        dns_o[...] = dns_in[...]
        mlc_o[...] = mlc_in[...]
        mln_o[...] = mln_in[...]
        mlm_o[...] = mlm_in[...]
        s5re_o[...] = s5re_in[...]
        s5im_o[...] = s5im_in[...]
    else:
        @pl.when(step == 0)
        def _():
            convbuf[0:n_state, :] = jnp.zeros((n_state, 3 * WIDTH), F32)
            dns_o[...] = jnp.zeros(dns_o.shape, F32)
            mlc_o[...] = jnp.zeros(mlc_o.shape, F32)
            mln_o[...] = jnp.zeros(mln_o.shape, F32)
            mlm_o[...] = jnp.full(mlm_o.shape, NEG, F32)
            s5re_o[...] = jnp.zeros(s5re_o.shape, F32)
            s5im_o[...] = jnp.zeros(s5im_o.shape, F32)

    h = _rms(x_ref[...], normw_ref[...])
    if n_pad:
        row = lax.broadcasted_iota(jnp.int32, (R, 1), 0)
        valid = row >= jnp.where(step == 0, n_pad, 0)
        h = jnp.where(valid, h, 0.0)
    hb = h.astype(BF16)

    for c in range(3):
        convbuf[n_state:, c * WIDTH:(c + 1) * WIDTH] = jnp.dot(
            hb, wa_ref[:, A_DN + c * WIDTH:A_DN + (c + 1) * WIDTH], preferred_element_type=F32)
    for c in range(3):
        p = jnp.dot(hb, wa_ref[:, A_ML + c * WIDTH:A_ML + (c + 1) * WIDTH], preferred_element_type=F32)
        if c == 1:
            p = p * (HEAD_DIM ** -0.5)
        for hh in range(HEADS):
            mlslab[c * HEADS + hh] = p[:, hh * HEAD_DIM:(hh + 1) * HEAD_DIM]
    u_scr[...] = jnp.dot(hb, wa_ref[:, A_S5:A_S5 + WIDTH], preferred_element_type=F32)

    z = jnp.dot(hb, wa_ref[:, A_SM:A_SM + LANES], preferred_element_type=F32) + gpar_ref[1:2, :]
    lane = lax.broadcasted_iota(jnp.int32, (R, LANES), 1)
    ig = z
    if n_pad:
        ig = jnp.where(valid, ig, NEG)
    gates = jnp.where(lane < SM_B, -jnp.exp(gpar_ref[0:1, :]) * _softplus(z),
                      jnp.where(lane < SM_I, _sigmoid(z),
                                jnp.where(lane < SM_F, ig,
                                          jnp.where(lane < SM_F + HEADS, -_softplus(-z), 0.0))))
    gslab[...] = gates

    for c in range(3):
        for hh in range(HEADS):
            col = c * WIDTH + hh * HEAD_DIM
            acc = convw_ref[0:1, col:col + HEAD_DIM] * convbuf[0:R, col:col + HEAD_DIM]
            for j in range(1, CONV_W):
                acc = acc + convw_ref[j:j + 1, col:col + HEAD_DIM] * convbuf[j * NB:j * NB + R, col:col + HEAD_DIM]
            y = _silu(acc)
            if c < 2:
                y = y * lax.rsqrt(jnp.sum(y * y, axis=-1, keepdims=True) + EPS)
            if c == 0:
                y = y * (HEAD_DIM ** -0.5)
            dnslab[c * HEADS + hh] = y
    new_conv = convbuf[R:R + n_state, :]
    conv_o[...] = new_conv
    convbuf[0:n_state, :] = new_conv

    ii = lax.broadcasted_iota(jnp.int32, (L, L), 0)
    jj = lax.broadcasted_iota(jnp.int32, (L, L), 1)
    tril = ii >= jj
    strict = ii > jj
    eye = ii == jj
    tril_b = jnp.where(tril, 1.0, 0.0).astype(BF16)
    lane_row = lax.broadcasted_iota(jnp.int32, (1, LANES), 1)
    n_double = max(1, (L - 1).bit_length())

    def row_of(col):
        return jnp.sum(jnp.where(eye, col, 0.0), axis=0, keepdims=True)

    def delta_head(out, q, kk, v, s, gb, cs, hh):
        gc = cs[:, SM_A + hh:SM_A + hh + 1]
        beta = gb[:, SM_B + hh:SM_B + hh + 1]
        gr = row_of(gc)
        decay = jnp.where(tril, jnp.exp(jnp.where(tril, gc - gr, 0.0)), 0.0)
        eg = jnp.exp(gc)
        kb = kk * beta
        kq = _dot_nt(jnp.concatenate([kb, q], axis=0), kk)
        yield
        a = jnp.where(strict, kq[:L] * decay, 0.0)
        attn = kq[L:] * decay
        xs = jnp.concatenate([v * beta, kb * eg], axis=-1)
        pw = -a
        for it in range(n_double):
            upd = _dot_wide_rhs(pw, xs) if it < WIDE_ROUNDS else _dot(pw, xs)
            if it + 1 < n_double:
                pw = _dot(pw, pw)
            yield
            xs = xs + upd
        u = xs[:, :HEAD_DIM]
        w = xs[:, HEAD_DIM:]
        sb = s.astype(BF16)
        ws = _dot(w, sb)
        qs = _dot(q * eg, sb)
        yield
        v_new = u - ws
        g_last = cs[L - 1:L, SM_A + hh:SM_A + hh + 1]
        av = _dot(attn, v_new)
        kv = _dot_tn(kk * jnp.exp(g_last - gc), v_new)
        yield
        out["o"] = qs + av
        out["s"] = s * jnp.exp(g_last) + kv

    def mlstm_head(out, q, kk, v, cm, nrm, m, gb, cs, hh):
        bc = cs[:, SM_F + hh:SM_F + hh + 1]
        igc = gb[:, SM_I + hh:SM_I + hh + 1]
        br = row_of(bc)
        igr = row_of(igc)
        inter = bc + m
        logd = jnp.where(tril, bc - br + igr, NEG)
        m_t = jnp.maximum(inter, jnp.max(logd, axis=-1, keepdims=True))
        m_new = m_t[L - 1:L, :]
        b_last = bc[L - 1:L, :]
        kw = kk * jnp.exp(b_last - bc + igc - m_new)
        qk = _dot_nt(q, kk)
        qc = _dot(q, cm)
        kv = _dot_tn(kw, v)
        yield
        sc = qk * jnp.exp(logd - m_t)
        scv = _dot(sc, v)
        yield
        w_inter = jnp.exp(inter - m_t)
        num = w_inter * qc + scv
        den = w_inter * jnp.sum(q * nrm, axis=-1, keepdims=True) + jnp.sum(sc, axis=-1, keepdims=True)
        carry_f = jnp.exp(b_last + m - m_new)
        out["o"] = num / jnp.maximum(jnp.abs(den), jnp.exp(-m_t))
        out["c"] = carry_f * cm + kv
        out["n"] = carry_f * nrm + jnp.sum(kw, axis=0, keepdims=True)
        out["m"] = m_new

    def seq_body(i, carry):
        seqs = [i * seq_per_iter + d for d in range(seq_per_iter)]
        chains, results = [], []
        for b in seqs:
            rows = pl.ds(b, L, stride=NB)
            gb = gslab[rows, :]
            m_row = mlm_o[pl.ds(b, 1), :]
            cs = _cumsum_rows(tril_b, gb)
            dn_out = [dict() for _ in range(HEADS)]
            ml_out = [dict() for _ in range(HEADS)]
            for hh in range(HEADS):
                dn = [dnslab[c * HEADS + hh, rows, :] for c in range(3)]
                ml = [mlslab[c * HEADS + hh, rows, :] for c in range(3)]
                chains.append(delta_head(dn_out[hh], *dn, dns_o[b, hh], gb, cs, hh))
                chains.append(mlstm_head(ml_out[hh], *ml, mlc_o[b, hh], mln_o[b, pl.ds(hh, 1), :],
                                         m_row[:, hh:hh + 1], gb, cs, hh))
            results.append((dn_out, ml_out, m_row))
        while chains:
            alive = []
            for ch in chains:
                try:
                    next(ch)
                    alive.append(ch)
                except StopIteration:
                    pass
            chains = alive
        for b, (dn_out, ml_out, m_row) in zip(seqs, results):
            rows = pl.ds(b, L, stride=NB)
            for hh in range(HEADS):
                o_ref[hh, rows, :] = dn_out[hh]["o"]
                o_ref[HEADS + hh, rows, :] = ml_out[hh]["o"]
                dns_o[b, hh] = dn_out[hh]["s"]
                mlc_o[b, hh] = ml_out[hh]["c"]
                mln_o[b, pl.ds(hh, 1), :] = ml_out[hh]["n"]
                m_row = jnp.where(lane_row == hh, ml_out[hh]["m"], m_row)
            mlm_o[pl.ds(b, 1), :] = m_row
        return carry

    lax.fori_loop(0, NB // seq_per_iter, seq_body, 0)

    for j in range(S5_BLOCKS):
        cols = slice(j * S5_BCOLS, (j + 1) * S5_BCOLS)
        ucols = slice(j * S5_BIN, (j + 1) * S5_BIN)
        ub = u_scr[:, ucols].astype(BF16)
        hsre[...] = jnp.dot(ub, wbre_ref[j], preferred_element_type=F32)
        hsim[...] = jnp.dot(ub, wbim_ref[j], preferred_element_type=F32)
        are = jnp.broadcast_to(lamre_ref[:, cols], (NB, S5_BCOLS))
        aim = jnp.broadcast_to(lamim_ref[:, cols], (NB, S5_BCOLS))

        def scan_body(t, c, are=are, aim=aim):
            hr, hi = c
            rws = pl.ds(pl.multiple_of(t * NB, NB), NB)
            nr = are * hr - aim * hi + hsre[rws, :]
            ni = are * hi + aim * hr + hsim[rws, :]
            hsre[rws, :] = nr
            hsim[rws, :] = ni
            return nr, ni

        hr, hi = lax.fori_loop(0, L, scan_body, (s5re_o[:, cols], s5im_o[:, cols]), unroll=S5_SCAN_UNROLL)
        s5re_o[:, cols] = hr
        s5im_o[:, cols] = hi
        o_ref[2 * HEADS + j] = (_dot(hsre[...], wcre_ref[j]) - _dot(hsim[...], wcim_ref[j])
                                + s5d_ref[:, ucols] * u_scr[:, ucols])


def _const_spec(shape, index):
    return pl.BlockSpec(shape, lambda s, index=index: index, pipeline_mode=pl.Buffered(1))


def _mixer_call(layer, x, p, init, chunk, n_pad):
    n_steps, R, _ = x.shape
    has_init = init is not None
    n_state = (CONV_W - 1) * NB
    n_seq = n_steps * NB if has_init else NB
    n_blk = n_steps if has_init else 1
    sidx = (lambda s: s) if has_init else (lambda s: 0)

    in_specs = [
        pl.BlockSpec((None, R, D_MODEL), lambda s: (s, 0, 0)),
        _const_spec((None, 1, D_MODEL), (layer, 0, 0)),
        _const_spec((None, D_MODEL, A_COLS), (layer, 0, 0)),
        _const_spec((None, CONV_W, 3 * WIDTH), (layer, 0, 0)),
        _const_spec((None, 2, LANES), (layer, 0, 0)),
        _const_spec((None, S5_BLOCKS, S5_BIN, S5_BCOLS), (layer, 0, 0, 0)),
        _const_spec((None, S5_BLOCKS, S5_BIN, S5_BCOLS), (layer, 0, 0, 0)),
        _const_spec((None, S5_BLOCKS, S5_BCOLS, S5_BIN), (layer, 0, 0, 0)),
        _const_spec((None, S5_BLOCKS, S5_BCOLS, S5_BIN), (layer, 0, 0, 0)),
        _const_spec((None, 1, S5_COLS), (layer, 0, 0)),
        _const_spec((None, 1, S5_COLS), (layer, 0, 0)),
        _const_spec((None, 1, WIDTH), (layer, 0, 0)),
    ]
    args = [x, p["norm_w"], p["wa"], p["conv_w"], p["gpar"], p["wb_re"], p["wb_im"], p["wc_re"], p["wc_im"],
            p["lam_re"], p["lam_im"], p["s5_d"]]
    if has_init:
        in_specs += [
            pl.BlockSpec((None, None, n_state, 3 * WIDTH), lambda s: (layer, s, 0, 0)),
            pl.BlockSpec((None, NB, HEADS, HEAD_DIM, HEAD_DIM), lambda s: (layer, s, 0, 0, 0)),
            pl.BlockSpec((None, NB, HEADS, HEAD_DIM, HEAD_DIM), lambda s: (layer, s, 0, 0, 0)),
            pl.BlockSpec((None, NB, HEADS, HEAD_DIM), lambda s: (layer, s, 0, 0)),
            pl.BlockSpec((None, NB, LANES), lambda s: (layer, s, 0)),
            pl.BlockSpec((None, NB, S5_COLS), lambda s: (layer, s, 0)),
            pl.BlockSpec((None, NB, S5_COLS), lambda s: (layer, s, 0)),
        ]
        args += list(init)

    out_shape = (
        jax.ShapeDtypeStruct((n_steps, 3 * HEADS, R, LANES), F32),
        jax.ShapeDtypeStruct((n_blk, n_state, 3 * WIDTH), F32),
        jax.ShapeDtypeStruct((n_seq, HEADS, HEAD_DIM, HEAD_DIM), F32),
        jax.ShapeDtypeStruct((n_seq, HEADS, HEAD_DIM, HEAD_DIM), F32),
        jax.ShapeDtypeStruct((n_seq, HEADS, HEAD_DIM), F32),
        jax.ShapeDtypeStruct((n_seq, LANES), F32),
        jax.ShapeDtypeStruct((n_seq, S5_COLS), F32),
        jax.ShapeDtypeStruct((n_seq, S5_COLS), F32),
    )
    out_specs = (
        pl.BlockSpec((None, 3 * HEADS, R, LANES), lambda s: (s, 0, 0, 0)),
        pl.BlockSpec((None, n_state, 3 * WIDTH), lambda s: (sidx(s), 0, 0)),
        pl.BlockSpec((NB, HEADS, HEAD_DIM, HEAD_DIM), lambda s: (sidx(s), 0, 0, 0)),
        pl.BlockSpec((NB, HEADS, HEAD_DIM, HEAD_DIM), lambda s: (sidx(s), 0, 0, 0)),
        pl.BlockSpec((NB, HEADS, HEAD_DIM), lambda s: (sidx(s), 0, 0)),
        pl.BlockSpec((NB, LANES), lambda s: (sidx(s), 0)),
        pl.BlockSpec((NB, S5_COLS), lambda s: (sidx(s), 0)),
        pl.BlockSpec((NB, S5_COLS), lambda s: (sidx(s), 0)),
    )
    scratch = [
        pltpu.VMEM((n_state + R, 3 * WIDTH), F32),
        pltpu.VMEM((3 * HEADS, R, LANES), F32),
        pltpu.VMEM((3 * HEADS, R, LANES), F32),
        pltpu.VMEM((R, LANES), F32),
        pltpu.VMEM((R, WIDTH), F32),
        pltpu.VMEM((R, S5_BCOLS), F32),
        pltpu.VMEM((R, S5_BCOLS), F32),
    ]
    return pl.pallas_call(
        functools.partial(_mixer_kernel, chunk=chunk, n_pad=n_pad, has_init=has_init),
        grid=(n_steps,), in_specs=in_specs, out_specs=out_specs, out_shape=out_shape,
        scratch_shapes=scratch, name="mixer_sample" if has_init else "mixer_prompt",
        compiler_params=pltpu.CompilerParams(dimension_semantics=("arbitrary",),
                                             vmem_limit_bytes=VMEM_LIMIT_BYTES),
    )(*args)


def _merge_kernel(*refs, n_pad, final):
    (x_ref, o_ref, normw_ref, wb_ref, bgate_ref, dnnw_ref, mlnw_ref, wglu_ref, bglu_ref,
     wbr_ref, wout_ref) = refs[:11]
    fin_ref = refs[11] if final else None
    out_ref = refs[-1]
    sb, R, _ = x_ref.shape
    rt = sb * R
    x = x_ref[...].reshape(rt, D_MODEL)
    hb = _rms(x, normw_ref[...]).astype(BF16)

    def slab(i):
        return o_ref[:, i].reshape(rt, LANES)

    def head_cols(a, hh):
        return a[:, hh * HEAD_DIM:(hh + 1) * HEAD_DIM]

    zall = jnp.dot(hb, wb_ref[:, 0:B_GATE], preferred_element_type=F32)
    dn_z, ml_o = zall[:, 0:WIDTH], zall[:, WIDTH:2 * WIDTH]
    ml_z, s5_z = zall[:, 2 * WIDTH:3 * WIDTH], zall[:, 3 * WIDTH:4 * WIDTH]

    o_dn = jnp.concatenate(
        [_rms(slab(hh), dnnw_ref[...]) * _silu(head_cols(dn_z, hh)) for hh in range(HEADS)], axis=-1)
    o_ml = jnp.concatenate(
        [_rms(slab(HEADS + hh), mlnw_ref[...]) * _sigmoid(head_cols(ml_o, hh)) * _silu(head_cols(ml_z, hh))
         for hh in range(HEADS)], axis=-1)
    y5 = _gelu_tanh(jnp.concatenate([slab(2 * HEADS + j) for j in range(S5_BLOCKS)], axis=-1))
    y5 = y5 * _sigmoid(_dot(y5, wglu_ref[...]) + bglu_ref[...])
    o_s5 = y5 * _silu(s5_z)

    mixed = None
    for i, ob in enumerate((o_dn, o_ml, o_s5)):
        gcols = slice(i * D_MODEL, (i + 1) * D_MODEL)
        gate = _sigmoid(jnp.dot(hb, wb_ref[:, B_GATE + i * D_MODEL:B_GATE + (i + 1) * D_MODEL],
                                preferred_element_type=F32) + bgate_ref[:, gcols])
        term = gate * _dot(ob, wbr_ref[i])
        mixed = term if mixed is None else mixed + term
    delta = _dot(mixed, wout_ref[...])
    if n_pad:
        row = lax.broadcasted_iota(jnp.int32, (rt, 1), 0)
        delta = jnp.where(row >= jnp.where(pl.program_id(0) == 0, n_pad, 0), delta, 0.0)
    xo = x + delta
    if final:
        xo = _rms(xo, fin_ref[...])
    out_ref[...] = xo.reshape(sb, R, D_MODEL)


def _merge_call(layer, x, o_raw, p, final_w, sb, n_pad):
    n_steps, R, _ = x.shape
    final = final_w is not None
    in_specs = [
        pl.BlockSpec((sb, R, D_MODEL), lambda s: (s, 0, 0)),
        pl.BlockSpec((sb, 3 * HEADS, R, LANES), lambda s: (s, 0, 0, 0)),
        _const_spec((None, 1, D_MODEL), (layer, 0, 0)),
        _const_spec((None, D_MODEL, B_COLS), (layer, 0, 0)),
        _const_spec((None, 1, N_BRANCH * D_MODEL), (layer, 0, 0)),
        _const_spec((None, 1, HEAD_DIM), (layer, 0, 0)),
        _const_spec((None, 1, HEAD_DIM), (layer, 0, 0)),
        _const_spec((None, WIDTH, WIDTH), (layer, 0, 0)),
        _const_spec((None, 1, WIDTH), (layer, 0, 0)),
        _const_spec((None, N_BRANCH, WIDTH, D_MODEL), (layer, 0, 0, 0)),
        _const_spec((None, D_MODEL, D_MODEL), (layer, 0, 0)),
    ]
    args = [x, o_raw, p["norm_w"], p["wb"], p["b_gate"], p["dn_norm_w"], p["ml_norm_w"], p["w_glu"], p["b_glu"],
            p["w_branch"], p["w_out"]]
    if final:
        in_specs.append(_const_spec((1, D_MODEL), (0, 0)))
        args.append(final_w)
    return pl.pallas_call(
        functools.partial(_merge_kernel, n_pad=n_pad, final=final),
        grid=(n_steps // sb,), in_specs=in_specs,
        out_specs=pl.BlockSpec((sb, R, D_MODEL), lambda s: (s, 0, 0)),
        out_shape=jax.ShapeDtypeStruct(x.shape, F32), name="merge",
        compiler_params=pltpu.CompilerParams(dimension_semantics=("arbitrary",),
                                             vmem_limit_bytes=VMEM_LIMIT_BYTES),
    )(*args)


def _time_major(a, nb=NB):
    n, t = a.shape[0], a.shape[1]
    a = a.reshape((n // nb, nb, t) + a.shape[2:])
    a = jnp.swapaxes(a, 1, 2)
    return a.reshape((n // nb, t * nb) + a.shape[3:])


def _seq_major(a, t, nb=NB):
    g = a.shape[0]
    a = a.reshape((g, t, nb) + a.shape[2:])
    a = jnp.swapaxes(a, 1, 2)
    return a.reshape((g * nb, t) + a.shape[3:])


def kernel(x_prompt, x_sample, state_dn_conv, state_dn_s, state_ml_c, state_ml_n, state_ml_m, state_s5_re, state_s5_im, meta_tokens, norm_w, w_in, b_gate, dn_conv_w, dn_a_log, dn_dt_bias, dn_norm_w, ml_bias_i, ml_bias_f, ml_norm_w, s5_lambda_re, s5_lambda_im, s5_log_dt, s5_b_re, s5_b_im, s5_c_re, s5_c_im, s5_d, s5_w_glu, s5_b_glu, w_branch_dn, w_branch_ml, w_branch_s5, w_out, final_norm_w):
    depth = w_in.shape[0]
    bsz, seq, _ = x_prompt.shape
    dec_b, dec_t, _ = x_sample.shape
    assert bsz == NB and dec_b % NB == 0 and seq % CHUNK == 0 and dec_t >= CONV_W - 1

    cuts = [0]
    for sz in (3 * WIDTH, WIDTH, HEADS, HEADS, 3 * WIDTH, WIDTH, WIDTH, HEADS, HEADS, WIDTH, WIDTH, N_BRANCH * D_MODEL):
        cuts.append(cuts[-1] + sz)
    seg = lambda i: w_in[:, :, cuts[i]:cuts[i + 1]]
    (dn_qkv, dn_z, dn_a, dn_b, ml_qkv, ml_o, ml_z, ml_i, ml_f, s5_u, s5_z, gate_w) = [seg(i) for i in range(12)]
    small = jnp.concatenate([dn_a, dn_b, ml_i, ml_f, jnp.zeros((depth, D_MODEL, LANES - 4 * HEADS), F32)], axis=-1)
    wa = jnp.concatenate([dn_qkv, ml_qkv, s5_u, small], axis=-1).astype(BF16)
    wb = jnp.concatenate([dn_z, ml_o, ml_z, s5_z, gate_w], axis=-1).astype(BF16)

    def lanes_row(parts):
        row = jnp.zeros((depth, LANES), F32)
        for off, val in parts:
            row = row.at[:, off:off + HEADS].set(val)
        return row

    gpar = jnp.stack([lanes_row([(SM_A, dn_a_log)]),
                      lanes_row([(SM_A, dn_dt_bias), (SM_I, ml_bias_i), (SM_F, ml_bias_f)])], axis=1)
    lam_re, lam_im, wb_re, wb_im, wc_re, wc_im = _s5_prepare(
        s5_lambda_re, s5_lambda_im, s5_log_dt, s5_b_re, s5_b_im, s5_c_re, s5_c_im)
    p = dict(
        norm_w=norm_w[:, None, :], wa=wa, wb=wb, conv_w=dn_conv_w, gpar=gpar,
        wb_re=wb_re, wb_im=wb_im, wc_re=wc_re, wc_im=wc_im, lam_re=lam_re, lam_im=lam_im,
        s5_d=s5_d[:, None, :], b_gate=b_gate[:, None, :], dn_norm_w=dn_norm_w[:, None, :],
        ml_norm_w=ml_norm_w[:, None, :], w_glu=s5_w_glu.astype(BF16), b_glu=s5_b_glu[:, None, :],
        w_branch=jnp.stack([w_branch_dn, w_branch_ml, w_branch_s5], axis=1).astype(BF16),
        w_out=w_out.astype(BF16))
    fin = final_norm_w[None, :]

    n_front = CHUNK - N_META
    xp = jnp.concatenate([jnp.zeros((bsz, n_front, D_MODEL), F32),
                          jnp.broadcast_to(meta_tokens[None], (bsz, N_META, D_MODEL)), x_prompt], axis=1)
    t_all = xp.shape[1]
    xp = jnp.swapaxes(xp, 0, 1).reshape(t_all // CHUNK, CHUNK * NB, D_MODEL)
    xs = _time_major(x_sample)

    n_grp = dec_b // NB
    conv_in = jnp.swapaxes(state_dn_conv.reshape(depth, n_grp, NB, CONV_W - 1, 3 * WIDTH), 2, 3)
    conv_in = conv_in.reshape(depth, n_grp, (CONV_W - 1) * NB, 3 * WIDTH)
    mlm_in = jnp.pad(state_ml_m, ((0, 0), (0, 0), (0, LANES - HEADS)))
    init = (conv_in, state_dn_s, state_ml_c, state_ml_n, mlm_in,
            state_s5_re.reshape(depth, dec_b, S5_COLS), state_s5_im.reshape(depth, dec_b, S5_COLS))

    p_states, s_states = [], []
    for layer in range(depth):
        last = layer == depth - 1
        res = _mixer_call(layer, xp, p, None, CHUNK, n_front * NB)
        p_states.append(res[1:])
        xp = _merge_call(layer, xp, res[0], p, fin if last else None, 1, n_front * NB)
        res = _mixer_call(layer, xs, p, init, dec_t, 0)
        s_states.append(res[1:])
        xs = _merge_call(layer, xs, res[0], p, fin if last else None, n_grp, 0)

    y_prompt = jnp.swapaxes(xp.reshape(t_all, bsz, D_MODEL), 0, 1)[:, CHUNK:]
    y_sample = _seq_major(xs, dec_t)

    def collect(states):
        st = [jnp.stack([s[i] for s in states]) for i in range(7)]
        conv, dns, mlc, mln, mlm, s5r, s5i = st
        n_seq = dns.shape[1]
        conv = conv.reshape(depth, n_seq // NB, CONV_W - 1, NB, 3 * WIDTH)
        conv = jnp.swapaxes(conv, 2, 3).reshape(depth, n_seq, CONV_W - 1, 3 * WIDTH)
        return (conv, dns, mlc, mln, mlm[:, :, :HEADS],
                s5r.reshape(depth, n_seq, S5_GROUPS, S5_STATE), s5i.reshape(depth, n_seq, S5_GROUPS, S5_STATE))

    return (y_prompt, y_sample) + collect(p_states) + collect(s_states)
```

```python
import functools

import jax
import jax.numpy as jnp
from jax import lax
from jax.experimental import pallas as pl
from jax.experimental.pallas import tpu as pltpu

F32 = jnp.float32
BF16 = jnp.bfloat16

D_MODEL = 1024
N_META = 16
CHUNK = 64
HEADS = 4
HEAD_DIM = 128
WIDTH = HEADS * HEAD_DIM
CONV_W = 4
S5_GROUP = 16
S5_GROUPS = 32
S5_STATE = 64
S5_COLS = S5_GROUPS * S5_STATE
S5_BLOCKS = 4
S5_BCOLS = S5_COLS // S5_BLOCKS
S5_BIN = WIDTH // S5_BLOCKS
N_BRANCH = 3
NB = 8
WIDE_ROUNDS = 3
S5_SCAN_UNROLL = 4
SEQ_PER_ITER_LONG = 2
SEQ_PER_ITER_SHORT = 4
LANES = 128
EPS = 1e-6
NEG = -1e30
VMEM_LIMIT_BYTES = 56 * 1024 * 1024

A_DN, A_ML, A_S5, A_SM = 0, 3 * WIDTH, 6 * WIDTH, 7 * WIDTH
A_COLS = 7 * WIDTH + LANES
B_GATE = 4 * WIDTH
B_COLS = 4 * WIDTH + N_BRANCH * D_MODEL
SM_A, SM_B, SM_I, SM_F = 0, HEADS, 2 * HEADS, 3 * HEADS


def _dot(a, b):
    return jnp.dot(a.astype(BF16), b.astype(BF16), preferred_element_type=F32)


def _dot_wide_rhs(a, b):
    ah = a.astype(BF16)
    bh = b.astype(BF16)
    bl = (b - bh.astype(F32)).astype(BF16)
    d = functools.partial(jnp.dot, preferred_element_type=F32)
    return d(ah, bh) + d(ah, bl)


def _dot_nt(a, b):
    return lax.dot_general(a.astype(BF16), b.astype(BF16), (((1,), (1,)), ((), ())),
                           preferred_element_type=F32)


def _dot_tn(a, b):
    return lax.dot_general(a.astype(BF16), b.astype(BF16), (((0,), (0,)), ((), ())),
                           preferred_element_type=F32)


def _sigmoid(x):
    return 1.0 / (1.0 + jnp.exp(-x))


def _silu(x):
    return x * _sigmoid(x)


def _softplus(x):
    return jnp.maximum(x, 0.0) + jnp.log1p(jnp.exp(-jnp.abs(x)))


def _gelu_tanh(x):
    return 0.5 * x * (1.0 + jnp.tanh(0.7978845608028654 * (x + 0.044715 * (x * x * x))))


def _rms(x, w):
    return x * lax.rsqrt(jnp.mean(x * x, axis=-1, keepdims=True) + EPS) * w


def _cumsum_rows(tril_b, x):
    hi = x.astype(BF16)
    r1 = x - hi.astype(F32)
    mid = r1.astype(BF16)
    lo = (r1 - mid.astype(F32)).astype(BF16)
    d = functools.partial(jnp.dot, preferred_element_type=F32)
    return d(tril_b, hi) + d(tril_b, mid) + d(tril_b, lo)


def _s5_prep_kernel(lre_ref, lim_ref, ldt_ref, bre_ref, bim_ref, olre_ref, olim_ref, obre_ref, obim_ref):
    lre, lim = lre_ref[...], lim_ref[...]
    dt = jnp.exp(ldt_ref[...])
    mag = jnp.exp(lre * dt)
    lbr = mag * jnp.cos(lim * dt)
    lbi = mag * jnp.sin(lim * dt)
    nr, ni = lbr - 1.0, lbi
    den = lre * lre + lim * lim
    cr = (nr * lre + ni * lim) / den
    ci = (ni * lre - nr * lim) / den
    bre, bim = bre_ref[...], bim_ref[...]
    olre_ref[...] = lbr
    olim_ref[...] = lbi
    obre_ref[...] = cr * bre - ci * bim
    obim_ref[...] = cr * bim + ci * bre


def _s5_prepare(lam_re, lam_im, log_dt, b_re, b_im, c_re, c_im):
    depth = lam_re.shape[0]
    rows = depth * S5_GROUPS * S5_GROUP

    def expand(a):
        return jnp.broadcast_to(a[:, :, None, :], (depth, S5_GROUPS, S5_GROUP, S5_STATE)).reshape(rows, S5_STATE)

    ldt = jnp.broadcast_to(log_dt[:, :, None, None], (depth, S5_GROUPS, S5_GROUP, S5_STATE)).reshape(rows, S5_STATE)
    bt = lambda b: jnp.swapaxes(b, 2, 3).reshape(rows, S5_STATE)
    sds = jax.ShapeDtypeStruct((rows, S5_STATE), F32)
    lbr, lbi, bbr, bbi = pl.pallas_call(
        _s5_prep_kernel, out_shape=(sds, sds, sds, sds), name="s5_prep",
    )(expand(lam_re), expand(lam_im), ldt, bt(b_re), bt(b_im))

    lam_row = lambda a: a.reshape(depth, S5_GROUPS, S5_GROUP, S5_STATE)[:, :, 0, :].reshape(depth, 1, S5_COLS)
    gpb = S5_GROUPS // S5_BLOCKS
    eye = jnp.eye(gpb, dtype=F32)

    def in_map(bb):
        bb = bb.reshape(depth, S5_BLOCKS, gpb, S5_GROUP, S5_STATE)
        w = bb[:, :, :, :, None, :] * eye[None, None, :, None, :, None]
        return w.reshape(depth, S5_BLOCKS, S5_BIN, S5_BCOLS).astype(BF16)

    def out_map(c):
        c = jnp.swapaxes(c, 2, 3).reshape(depth, S5_BLOCKS, gpb, S5_STATE, S5_GROUP)
        w = c[:, :, :, :, None, :] * eye[None, None, :, None, :, None]
        return w.reshape(depth, S5_BLOCKS, S5_BCOLS, S5_BIN).astype(BF16)

    return lam_row(lbr), lam_row(lbi), in_map(bbr), in_map(bbi), out_map(c_re), out_map(c_im)


def _mixer_kernel(*refs, chunk, n_pad, has_init):
    L = chunk
    seq_per_iter = SEQ_PER_ITER_LONG if L >= CHUNK else SEQ_PER_ITER_SHORT
    R = L * NB
    (x_ref, normw_ref, wa_ref, convw_ref, gpar_ref, wbre_ref, wbim_ref, wcre_ref, wcim_ref,
     lamre_ref, lamim_ref, s5d_ref) = refs[:12]
    k = 12
    if has_init:
        conv_in, dns_in, mlc_in, mln_in, mlm_in, s5re_in, s5im_in = refs[k:k + 7]
        k += 7
    o_ref, conv_o, dns_o, mlc_o, mln_o, mlm_o, s5re_o, s5im_o = refs[k:k + 8]
    convbuf, dnslab, mlslab, gslab, u_scr, hsre, hsim = refs[k + 8:]
    step = pl.program_id(0)
    n_state = (CONV_W - 1) * NB

    if has_init:
        convbuf[0:n_state, :] = conv_in[...]
        dns_o[...] = dns_in[...]
        mlc_o[...] = mlc_in[...]
        mln_o[...] = mln_in[...]
        mlm_o[...] = mlm_in[...]
        s5re_o[...] = s5re_in[...]
        s5im_o[...] = s5im_in[...]
    else:
        @pl.when(step == 0)
        def _():
            convbuf[0:n_state, :] = jnp.zeros((n_state, 3 * WIDTH), F32)
            dns_o[...] = jnp.zeros(dns_o.shape, F32)
            mlc_o[...] = jnp.zeros(mlc_o.shape, F32)
            mln_o[...] = jnp.zeros(mln_o.shape, F32)
            mlm_o[...] = jnp.full(mlm_o.shape, NEG, F32)
            s5re_o[...] = jnp.zeros(s5re_o.shape, F32)
            s5im_o[...] = jnp.zeros(s5im_o.shape, F32)

    h = _rms(x_ref[...], normw_ref[...])
    if n_pad:
        row = lax.broadcasted_iota(jnp.int32, (R, 1), 0)
        valid = row >= jnp.where(step == 0, n_pad, 0)
        h = jnp.where(valid, h, 0.0)
    hb = h.astype(BF16)

    for c in range(3):
        convbuf[n_state:, c * WIDTH:(c + 1) * WIDTH] = jnp.dot(
            hb, wa_ref[:, A_DN + c * WIDTH:A_DN + (c + 1) * WIDTH], preferred_element_type=F32)
    for c in range(3):
        p = jnp.dot(hb, wa_ref[:, A_ML + c * WIDTH:A_ML + (c + 1) * WIDTH], preferred_element_type=F32)
        if c == 1:
            p = p * (HEAD_DIM ** -0.5)
        for hh in range(HEADS):
            mlslab[c * HEADS + hh] = p[:, hh * HEAD_DIM:(hh + 1) * HEAD_DIM]
    u_scr[...] = jnp.dot(hb, wa_ref[:, A_S5:A_S5 + WIDTH], preferred_element_type=F32)

    z = jnp.dot(hb, wa_ref[:, A_SM:A_SM + LANES], preferred_element_type=F32) + gpar_ref[1:2, :]
    lane = lax.broadcasted_iota(jnp.int32, (R, LANES), 1)
    ig = z
    if n_pad:
        ig = jnp.where(valid, ig, NEG)
    gates = jnp.where(lane < SM_B, -jnp.exp(gpar_ref[0:1, :]) * _softplus(z),
                      jnp.where(lane < SM_I, _sigmoid(z),
                                jnp.where(lane < SM_F, ig,
                                          jnp.where(lane < SM_F + HEADS, -_softplus(-z), 0.0))))
    gslab[...] = gates

    for c in range(3):
        for hh in range(HEADS):
            col = c * WIDTH + hh * HEAD_DIM
            acc = convw_ref[0:1, col:col + HEAD_DIM] * convbuf[0:R, col:col + HEAD_DIM]
            for j in range(1, CONV_W):
                acc = acc + convw_ref[j:j + 1, col:col + HEAD_DIM] * convbuf[j * NB:j * NB + R, col:col + HEAD_DIM]
            y = _silu(acc)
            if c < 2:
                y = y * lax.rsqrt(jnp.sum(y * y, axis=-1, keepdims=True) + EPS)
            if c == 0:
                y = y * (HEAD_DIM ** -0.5)
            dnslab[c * HEADS + hh] = y
    new_conv = convbuf[R:R + n_state, :]
    conv_o[...] = new_conv
    convbuf[0:n_state, :] = new_conv

    ii = lax.broadcasted_iota(jnp.int32, (L, L), 0)
    jj = lax.broadcasted_iota(jnp.int32, (L, L), 1)
    tril = ii >= jj
    strict = ii > jj
    eye = ii == jj
    tril_b = jnp.where(tril, 1.0, 0.0).astype(BF16)
    lane_row = lax.broadcasted_iota(jnp.int32, (1, LANES), 1)
    n_double = max(1, (L - 1).bit_length())

    def row_of(col):
        return jnp.sum(jnp.where(eye, col, 0.0), axis=0, keepdims=True)

    def delta_head(out, q, kk, v, s, gb, cs, hh):
        gc = cs[:, SM_A + hh:SM_A + hh + 1]
        beta = gb[:, SM_B + hh:SM_B + hh + 1]
        gr = row_of(gc)
        decay = jnp.where(tril, jnp.exp(jnp.where(tril, gc - gr, 0.0)), 0.0)
        eg = jnp.exp(gc)
        kb = kk * beta
        kq = _dot_nt(jnp.concatenate([kb, q], axis=0), kk)
        yield
        a = jnp.where(strict, kq[:L] * decay, 0.0)
        attn = kq[L:] * decay
        xs = jnp.concatenate([v * beta, kb * eg], axis=-1)
        pw = -a
        for it in range(n_double):
            upd = _dot_wide_rhs(pw, xs) if it < WIDE_ROUNDS else _dot(pw, xs)
            if it + 1 < n_double:
                pw = _dot(pw, pw)
            yield
            xs = xs + upd
        u = xs[:, :HEAD_DIM]
        w = xs[:, HEAD_DIM:]
        sb = s.astype(BF16)
        ws = _dot(w, sb)
        qs = _dot(q * eg, sb)
        yield
        v_new = u - ws
        g_last = cs[L - 1:L, SM_A + hh:SM_A + hh + 1]
        av = _dot(attn, v_new)
        kv = _dot_tn(kk * jnp.exp(g_last - gc), v_new)
        yield
        out["o"] = qs + av
        out["s"] = s * jnp.exp(g_last) + kv

    def mlstm_head(out, q, kk, v, cm, nrm, m, gb, cs, hh):
        bc = cs[:, SM_F + hh:SM_F + hh + 1]
        igc = gb[:, SM_I + hh:SM_I + hh + 1]
        br = row_of(bc)
        igr = row_of(igc)
        inter = bc + m
        logd = jnp.where(tril, bc - br + igr, NEG)
        m_t = jnp.maximum(inter, jnp.max(logd, axis=-1, keepdims=True))
        m_new = m_t[L - 1:L, :]
        b_last = bc[L - 1:L, :]
        qk = _dot_nt(q, kk)
        qc = _dot(q, cm)
        yield
        kw = kk * jnp.exp(b_last - bc + igc - m_new)
        kv = _dot_tn(kw, v)
        sc = qk * jnp.exp(logd - m_t)
        scv = _dot(sc, v)
        yield
        w_inter = jnp.exp(inter - m_t)
        num = w_inter * qc + scv
        den = w_inter * jnp.sum(q * nrm, axis=-1, keepdims=True) + jnp.sum(sc, axis=-1, keepdims=True)
        carry_f = jnp.exp(b_last + m - m_new)
        out["o"] = num / jnp.maximum(jnp.abs(den), jnp.exp(-m_t))
        out["c"] = carry_f * cm + kv
        out["n"] = carry_f * nrm + jnp.sum(kw, axis=0, keepdims=True)
        out["m"] = m_new

    def seq_body(i, carry, *, per_iter, do_dn, do_ml):
        seqs = [i * per_iter + d for d in range(per_iter)]
        chains, results = [], []
        for b in seqs:
            rows = pl.ds(b, L, stride=NB)
            gb = gslab[rows, :]
            m_row = mlm_o[pl.ds(b, 1), :]
            cs = _cumsum_rows(tril_b, gb)
            dn_out = [dict() for _ in range(HEADS)]
            ml_out = [dict() for _ in range(HEADS)]
            for hh in range(HEADS):
                if do_dn:
                    dn = [dnslab[c * HEADS + hh, rows, :] for c in range(3)]
                    chains.append(delta_head(dn_out[hh], *dn, dns_o[b, hh], gb, cs, hh))
                if do_ml:
                    ml = [mlslab[c * HEADS + hh, rows, :] for c in range(3)]
                    chains.append(mlstm_head(ml_out[hh], *ml, mlc_o[b, hh], mln_o[b, pl.ds(hh, 1), :],
                                             m_row[:, hh:hh + 1], gb, cs, hh))
            results.append((dn_out, ml_out, m_row))
        while chains:
            alive = []
            for ch in chains:
                try:
                    next(ch)
                    alive.append(ch)
                except StopIteration:
                    pass
            chains = alive
        for b, (dn_out, ml_out, m_row) in zip(seqs, results):
            rows = pl.ds(b, L, stride=NB)
            for hh in range(HEADS):
                if do_dn:
                    o_ref[hh, rows, :] = dn_out[hh]["o"]
                    dns_o[b, hh] = dn_out[hh]["s"]
                if do_ml:
                    o_ref[HEADS + hh, rows, :] = ml_out[hh]["o"]
                    mlc_o[b, hh] = ml_out[hh]["c"]
                    mln_o[b, pl.ds(hh, 1), :] = ml_out[hh]["n"]
                    m_row = jnp.where(lane_row == hh, ml_out[hh]["m"], m_row)
            if do_ml:
                mlm_o[pl.ds(b, 1), :] = m_row
        return carry

    lax.fori_loop(0, NB // seq_per_iter,
                  functools.partial(seq_body, per_iter=seq_per_iter, do_dn=True, do_ml=True), 0)

    for j in range(S5_BLOCKS):
        cols = slice(j * S5_BCOLS, (j + 1) * S5_BCOLS)
        ucols = slice(j * S5_BIN, (j + 1) * S5_BIN)
        ub = u_scr[:, ucols].astype(BF16)
        hsre[...] = jnp.dot(ub, wbre_ref[j], preferred_element_type=F32)
        hsim[...] = jnp.dot(ub, wbim_ref[j], preferred_element_type=F32)
        are = jnp.broadcast_to(lamre_ref[:, cols], (NB, S5_BCOLS))
        aim = jnp.broadcast_to(lamim_ref[:, cols], (NB, S5_BCOLS))

        def scan_body(t, c, are=are, aim=aim):
            hr, hi = c
            rws = pl.ds(pl.multiple_of(t * NB, NB), NB)
            nr = are * hr - aim * hi + hsre[rws, :]
            ni = are * hi + aim * hr + hsim[rws, :]
            hsre[rws, :] = nr
            hsim[rws, :] = ni
            return nr, ni

        hr, hi = lax.fori_loop(0, L, scan_body, (s5re_o[:, cols], s5im_o[:, cols]), unroll=S5_SCAN_UNROLL)
        s5re_o[:, cols] = hr
        s5im_o[:, cols] = hi
        o_ref[2 * HEADS + j] = (_dot(hsre[...], wcre_ref[j]) - _dot(hsim[...], wcim_ref[j])
                                + s5d_ref[:, ucols] * u_scr[:, ucols])


def _const_spec(shape, index):
    return pl.BlockSpec(shape, lambda s, index=index: index, pipeline_mode=pl.Buffered(1))


def _mixer_call(layer, x, p, init, chunk, n_pad):
    n_steps, R, _ = x.shape
    has_init = init is not None
    n_state = (CONV_W - 1) * NB
    n_seq = n_steps * NB if has_init else NB
    n_blk = n_steps if has_init else 1
    sidx = (lambda s: s) if has_init else (lambda s: 0)

    in_specs = [
        pl.BlockSpec((None, R, D_MODEL), lambda s: (s, 0, 0)),
        _const_spec((None, 1, D_MODEL), (layer, 0, 0)),
        _const_spec((None, D_MODEL, A_COLS), (layer, 0, 0)),
        _const_spec((None, CONV_W, 3 * WIDTH), (layer, 0, 0)),
        _const_spec((None, 2, LANES), (layer, 0, 0)),
        _const_spec((None, S5_BLOCKS, S5_BIN, S5_BCOLS), (layer, 0, 0, 0)),
        _const_spec((None, S5_BLOCKS, S5_BIN, S5_BCOLS), (layer, 0, 0, 0)),
        _const_spec((None, S5_BLOCKS, S5_BCOLS, S5_BIN), (layer, 0, 0, 0)),
        _const_spec((None, S5_BLOCKS, S5_BCOLS, S5_BIN), (layer, 0, 0, 0)),
        _const_spec((None, 1, S5_COLS), (layer, 0, 0)),
        _const_spec((None, 1, S5_COLS), (layer, 0, 0)),
        _const_spec((None, 1, WIDTH), (layer, 0, 0)),
    ]
    args = [x, p["norm_w"], p["wa"], p["conv_w"], p["gpar"], p["wb_re"], p["wb_im"], p["wc_re"], p["wc_im"],
            p["lam_re"], p["lam_im"], p["s5_d"]]
    if has_init:
        in_specs += [
            pl.BlockSpec((None, None, n_state, 3 * WIDTH), lambda s: (layer, s, 0, 0)),
            pl.BlockSpec((None, NB, HEADS, HEAD_DIM, HEAD_DIM), lambda s: (layer, s, 0, 0, 0)),
            pl.BlockSpec((None, NB, HEADS, HEAD_DIM, HEAD_DIM), lambda s: (layer, s, 0, 0, 0)),
            pl.BlockSpec((None, NB, HEADS, HEAD_DIM), lambda s: (layer, s, 0, 0)),
            pl.BlockSpec((None, NB, LANES), lambda s: (layer, s, 0)),
            pl.BlockSpec((None, NB, S5_COLS), lambda s: (layer, s, 0)),
            pl.BlockSpec((None, NB, S5_COLS), lambda s: (layer, s, 0)),
        ]
        args += list(init)

    out_shape = (
        jax.ShapeDtypeStruct((n_steps, 3 * HEADS, R, LANES), F32),
        jax.ShapeDtypeStruct((n_blk, n_state, 3 * WIDTH), F32),
        jax.ShapeDtypeStruct((n_seq, HEADS, HEAD_DIM, HEAD_DIM), F32),
        jax.ShapeDtypeStruct((n_seq, HEADS, HEAD_DIM, HEAD_DIM), F32),
        jax.ShapeDtypeStruct((n_seq, HEADS, HEAD_DIM), F32),
        jax.ShapeDtypeStruct((n_seq, LANES), F32),
        jax.ShapeDtypeStruct((n_seq, S5_COLS), F32),
        jax.ShapeDtypeStruct((n_seq, S5_COLS), F32),
    )
    out_specs = (
        pl.BlockSpec((None, 3 * HEADS, R, LANES), lambda s: (s, 0, 0, 0)),
        pl.BlockSpec((None, n_state, 3 * WIDTH), lambda s: (sidx(s), 0, 0)),
        pl.BlockSpec((NB, HEADS, HEAD_DIM, HEAD_DIM), lambda s: (sidx(s), 0, 0, 0)),
        pl.BlockSpec((NB, HEADS, HEAD_DIM, HEAD_DIM), lambda s: (sidx(s), 0, 0, 0)),
        pl.BlockSpec((NB, HEADS, HEAD_DIM), lambda s: (sidx(s), 0, 0)),
        pl.BlockSpec((NB, LANES), lambda s: (sidx(s), 0)),
        pl.BlockSpec((NB, S5_COLS), lambda s: (sidx(s), 0)),
        pl.BlockSpec((NB, S5_COLS), lambda s: (sidx(s), 0)),
    )
    scratch = [
        pltpu.VMEM((n_state + R, 3 * WIDTH), F32),
        pltpu.VMEM((3 * HEADS, R, LANES), F32),
        pltpu.VMEM((3 * HEADS, R, LANES), F32),
        pltpu.VMEM((R, LANES), F32),
        pltpu.VMEM((R, WIDTH), F32),
        pltpu.VMEM((R, S5_BCOLS), F32),
        pltpu.VMEM((R, S5_BCOLS), F32),
    ]
    return pl.pallas_call(
        functools.partial(_mixer_kernel, chunk=chunk, n_pad=n_pad, has_init=has_init),
        grid=(n_steps,), in_specs=in_specs, out_specs=out_specs, out_shape=out_shape,
        scratch_shapes=scratch, name="mixer_sample" if has_init else "mixer_prompt",
        compiler_params=pltpu.CompilerParams(dimension_semantics=("arbitrary",),
                                             vmem_limit_bytes=VMEM_LIMIT_BYTES),
    )(*args)


def _merge_kernel(*refs, n_pad, final):
    (x_ref, o_ref, normw_ref, wb_ref, bgate_ref, dnnw_ref, mlnw_ref, wglu_ref, bglu_ref,
     wbr_ref, wout_ref) = refs[:11]
    fin_ref = refs[11] if final else None
    out_ref = refs[-1]
    sb, R, _ = x_ref.shape
    rt = sb * R
    x = x_ref[...].reshape(rt, D_MODEL)
    hb = _rms(x, normw_ref[...]).astype(BF16)

    def slab(i):
        return o_ref[:, i].reshape(rt, LANES)

    def head_cols(a, hh):
        return a[:, hh * HEAD_DIM:(hh + 1) * HEAD_DIM]

    zall = jnp.dot(hb, wb_ref[:, 0:B_GATE], preferred_element_type=F32)
    dn_z, ml_o = zall[:, 0:WIDTH], zall[:, WIDTH:2 * WIDTH]
    ml_z, s5_z = zall[:, 2 * WIDTH:3 * WIDTH], zall[:, 3 * WIDTH:4 * WIDTH]

    o_dn = jnp.concatenate(
        [_rms(slab(hh), dnnw_ref[...]) * _silu(head_cols(dn_z, hh)) for hh in range(HEADS)], axis=-1)
    o_ml = jnp.concatenate(
        [_rms(slab(HEADS + hh), mlnw_ref[...]) * _sigmoid(head_cols(ml_o, hh)) * _silu(head_cols(ml_z, hh))
         for hh in range(HEADS)], axis=-1)
    y5 = _gelu_tanh(jnp.concatenate([slab(2 * HEADS + j) for j in range(S5_BLOCKS)], axis=-1))
    y5 = y5 * _sigmoid(_dot(y5, wglu_ref[...]) + bglu_ref[...])
    o_s5 = y5 * _silu(s5_z)

    mixed = None
    for i, ob in enumerate((o_dn, o_ml, o_s5)):
        gcols = slice(i * D_MODEL, (i + 1) * D_MODEL)
        gate = _sigmoid(jnp.dot(hb, wb_ref[:, B_GATE + i * D_MODEL:B_GATE + (i + 1) * D_MODEL],
                                preferred_element_type=F32) + bgate_ref[:, gcols])
        term = gate * _dot(ob, wbr_ref[i])
        mixed = term if mixed is None else mixed + term
    delta = _dot(mixed, wout_ref[...])
    if n_pad:
        row = lax.broadcasted_iota(jnp.int32, (rt, 1), 0)
        delta = jnp.where(row >= jnp.where(pl.program_id(0) == 0, n_pad, 0), delta, 0.0)
    xo = x + delta
    if final:
        xo = _rms(xo, fin_ref[...])
    out_ref[...] = xo.reshape(sb, R, D_MODEL)


def _merge_call(layer, x, o_raw, p, final_w, sb, n_pad):
    n_steps, R, _ = x.shape
    final = final_w is not None
    in_specs = [
        pl.BlockSpec((sb, R, D_MODEL), lambda s: (s, 0, 0)),
        pl.BlockSpec((sb, 3 * HEADS, R, LANES), lambda s: (s, 0, 0, 0)),
        _const_spec((None, 1, D_MODEL), (layer, 0, 0)),
        _const_spec((None, D_MODEL, B_COLS), (layer, 0, 0)),
        _const_spec((None, 1, N_BRANCH * D_MODEL), (layer, 0, 0)),
        _const_spec((None, 1, HEAD_DIM), (layer, 0, 0)),
        _const_spec((None, 1, HEAD_DIM), (layer, 0, 0)),
        _const_spec((None, WIDTH, WIDTH), (layer, 0, 0)),
        _const_spec((None, 1, WIDTH), (layer, 0, 0)),
        _const_spec((None, N_BRANCH, WIDTH, D_MODEL), (layer, 0, 0, 0)),
        _const_spec((None, D_MODEL, D_MODEL), (layer, 0, 0)),
    ]
    args = [x, o_raw, p["norm_w"], p["wb"], p["b_gate"], p["dn_norm_w"], p["ml_norm_w"], p["w_glu"], p["b_glu"],
            p["w_branch"], p["w_out"]]
    if final:
        in_specs.append(_const_spec((1, D_MODEL), (0, 0)))
        args.append(final_w)
    return pl.pallas_call(
        functools.partial(_merge_kernel, n_pad=n_pad, final=final),
        grid=(n_steps // sb,), in_specs=in_specs,
        out_specs=pl.BlockSpec((sb, R, D_MODEL), lambda s: (s, 0, 0)),
        out_shape=jax.ShapeDtypeStruct(x.shape, F32), name="merge",
        compiler_params=pltpu.CompilerParams(dimension_semantics=("arbitrary",),
                                             vmem_limit_bytes=VMEM_LIMIT_BYTES),
    )(*args)


def _time_major(a, nb=NB):
    n, t = a.shape[0], a.shape[1]
    a = a.reshape((n // nb, nb, t) + a.shape[2:])
    a = jnp.swapaxes(a, 1, 2)
    return a.reshape((n // nb, t * nb) + a.shape[3:])


def _seq_major(a, t, nb=NB):
    g = a.shape[0]
    a = a.reshape((g, t, nb) + a.shape[2:])
    a = jnp.swapaxes(a, 1, 2)
    return a.reshape((g * nb, t) + a.shape[3:])


def kernel(x_prompt, x_sample, state_dn_conv, state_dn_s, state_ml_c, state_ml_n, state_ml_m, state_s5_re, state_s5_im, meta_tokens, norm_w, w_in, b_gate, dn_conv_w, dn_a_log, dn_dt_bias, dn_norm_w, ml_bias_i, ml_bias_f, ml_norm_w, s5_lambda_re, s5_lambda_im, s5_log_dt, s5_b_re, s5_b_im, s5_c_re, s5_c_im, s5_d, s5_w_glu, s5_b_glu, w_branch_dn, w_branch_ml, w_branch_s5, w_out, final_norm_w):
    depth = w_in.shape[0]
    bsz, seq, _ = x_prompt.shape
    dec_b, dec_t, _ = x_sample.shape
    assert bsz == NB and dec_b % NB == 0 and seq % CHUNK == 0 and dec_t >= CONV_W - 1

    cuts = [0]
    for sz in (3 * WIDTH, WIDTH, HEADS, HEADS, 3 * WIDTH, WIDTH, WIDTH, HEADS, HEADS, WIDTH, WIDTH, N_BRANCH * D_MODEL):
        cuts.append(cuts[-1] + sz)
    seg = lambda i: w_in[:, :, cuts[i]:cuts[i + 1]]
    (dn_qkv, dn_z, dn_a, dn_b, ml_qkv, ml_o, ml_z, ml_i, ml_f, s5_u, s5_z, gate_w) = [seg(i) for i in range(12)]
    small = jnp.concatenate([dn_a, dn_b, ml_i, ml_f, jnp.zeros((depth, D_MODEL, LANES - 4 * HEADS), F32)], axis=-1)
    wa = jnp.concatenate([dn_qkv, ml_qkv, s5_u, small], axis=-1).astype(BF16)
    wb = jnp.concatenate([dn_z, ml_o, ml_z, s5_z, gate_w], axis=-1).astype(BF16)

    def lanes_row(parts):
        row = jnp.zeros((depth, LANES), F32)
        for off, val in parts:
            row = row.at[:, off:off + HEADS].set(val)
        return row

    gpar = jnp.stack([lanes_row([(SM_A, dn_a_log)]),
                      lanes_row([(SM_A, dn_dt_bias), (SM_I, ml_bias_i), (SM_F, ml_bias_f)])], axis=1)
    lam_re, lam_im, wb_re, wb_im, wc_re, wc_im = _s5_prepare(
        s5_lambda_re, s5_lambda_im, s5_log_dt, s5_b_re, s5_b_im, s5_c_re, s5_c_im)
    p = dict(
        norm_w=norm_w[:, None, :], wa=wa, wb=wb, conv_w=dn_conv_w, gpar=gpar,
        wb_re=wb_re, wb_im=wb_im, wc_re=wc_re, wc_im=wc_im, lam_re=lam_re, lam_im=lam_im,
        s5_d=s5_d[:, None, :], b_gate=b_gate[:, None, :], dn_norm_w=dn_norm_w[:, None, :],
        ml_norm_w=ml_norm_w[:, None, :], w_glu=s5_w_glu.astype(BF16), b_glu=s5_b_glu[:, None, :],
        w_branch=jnp.stack([w_branch_dn, w_branch_ml, w_branch_s5], axis=1).astype(BF16),
        w_out=w_out.astype(BF16))
    fin = final_norm_w[None, :]

    n_front = CHUNK - N_META
    xp = jnp.concatenate([jnp.zeros((bsz, n_front, D_MODEL), F32),
                          jnp.broadcast_to(meta_tokens[None], (bsz, N_META, D_MODEL)), x_prompt], axis=1)
    t_all = xp.shape[1]
    xp = jnp.swapaxes(xp, 0, 1).reshape(t_all // CHUNK, CHUNK * NB, D_MODEL)
    xs = _time_major(x_sample)

    n_grp = dec_b // NB
    conv_in = jnp.swapaxes(state_dn_conv.reshape(depth, n_grp, NB, CONV_W - 1, 3 * WIDTH), 2, 3)
    conv_in = conv_in.reshape(depth, n_grp, (CONV_W - 1) * NB, 3 * WIDTH)
    mlm_in = jnp.pad(state_ml_m, ((0, 0), (0, 0), (0, LANES - HEADS)))
    init = (conv_in, state_dn_s, state_ml_c, state_ml_n, mlm_in,
            state_s5_re.reshape(depth, dec_b, S5_COLS), state_s5_im.reshape(depth, dec_b, S5_COLS))

    p_states, s_states = [], []
    for layer in range(depth):
        last = layer == depth - 1
        res = _mixer_call(layer, xp, p, None, CHUNK, n_front * NB)
        p_states.append(res[1:])
        xp = _merge_call(layer, xp, res[0], p, fin if last else None, 1, n_front * NB)
        res = _mixer_call(layer, xs, p, init, dec_t, 0)
        s_states.append(res[1:])
        xs = _merge_call(layer, xs, res[0], p, fin if last else None, n_grp, 0)

    y_prompt = jnp.swapaxes(xp.reshape(t_all, bsz, D_MODEL), 0, 1)[:, CHUNK:]
    y_sample = _seq_major(xs, dec_t)

    def collect(states):
        st = [jnp.stack([s[i] for s in states]) for i in range(7)]
        conv, dns, mlc, mln, mlm, s5r, s5i = st
        n_seq = dns.shape[1]
        conv = conv.reshape(depth, n_seq // NB, CONV_W - 1, NB, 3 * WIDTH)
        conv = jnp.swapaxes(conv, 2, 3).reshape(depth, n_seq, CONV_W - 1, 3 * WIDTH)
        return (conv, dns, mlc, mln, mlm[:, :, :HEADS],
                s5r.reshape(depth, n_seq, S5_GROUPS, S5_STATE), s5i.reshape(depth, n_seq, S5_GROUPS, S5_STATE))

    return (y_prompt, y_sample) + collect(p_states) + collect(s_states)
```
